```python
import numpy as np
import jax
import jax.numpy as jnp
from jax import lax

D_MODEL = 2048
BATCH = 4
SEQ = 8192
DEPTH = 1
DEC_BATCH = 32
DEC_SEQ = 32
PAST_LEN = 2048

CHUNK = 64
GLA_HEADS = 4
GLA_DK = 128
GLA_DV = 256
GLA_GATE_RANK = 16
GLA_GATE_NORM = 16.0
SWA_Q_HEADS = 16
SWA_KV_HEADS = 4
SWA_GROUP = SWA_Q_HEADS // SWA_KV_HEADS
HEAD_DIM = 64
WINDOW = 128
WINDOW_CHUNKS = WINDOW // CHUNK
ALIBI_MAX_BIAS = 8.0
MIX_WIDTH = GLA_HEADS * GLA_DV + SWA_Q_HEADS * HEAD_DIM
IN_WIDTHS = (GLA_HEADS * GLA_DK, GLA_HEADS * GLA_DK, GLA_HEADS * GLA_DV, GLA_GATE_RANK,
             GLA_HEADS * GLA_DV, SWA_Q_HEADS * HEAD_DIM, SWA_KV_HEADS * HEAD_DIM,
             SWA_KV_HEADS * HEAD_DIM)
IN_COLS = sum(IN_WIDTHS)
N_EXPERTS = 32
TOP_K = 4
D_FF = 2048
SWIGLU_LIMIT = 7.0
SWIGLU_ALPHA = 1.702
MOE_BLOCK = 128
RMS_EPS = 1e-5
NEG_INF = -1e30

kernel_name = 'hymba_gla_swa_moe_stream_step'


def rms_norm(x, w):
    xf = x.astype(jnp.float32)
    y = xf * lax.rsqrt(jnp.mean(xf * xf, axis=-1, keepdims=True) + RMS_EPS)
    return (y * w.astype(jnp.float32)).astype(x.dtype)


def gla_recurrence(q, k, v, log_f, s0, chunk):
    B, T, H, _ = q.shape
    nc = T // chunk

    def to_chunks(a):
        return a.reshape(B, nc, chunk, H, a.shape[-1]).swapaxes(0, 1).astype(jnp.float32)

    def step(s, inp):
        qc, kc, vc, fc = inp
        g = jnp.cumsum(fc, axis=1)
        g_end = g[:, -1]
        k_dec = kc * jnp.exp(g_end[:, None] - g)
        s_new = jnp.exp(g_end)[..., None] * s + jnp.einsum('bchk,bchv->bhkv', k_dec, vc)
        o = jnp.einsum('bchk,bhkv->bchv', qc, s_new)
        return s_new, o

    s_fin, o = lax.scan(step, s0.astype(jnp.float32),
                        (to_chunks(q), to_chunks(k), to_chunks(v), to_chunks(log_f)))
    o = o.swapaxes(0, 1).reshape(B, T, H, GLA_DV)
    return o, s_fin


def alibi_bias(q_off, lq, lk):
    slopes = jnp.exp2(-ALIBI_MAX_BIAS * jnp.arange(1, SWA_Q_HEADS + 1, dtype=jnp.float32) / SWA_Q_HEADS)
    slopes = slopes.reshape(SWA_KV_HEADS, SWA_GROUP)
    dist = jnp.abs(q_off + jnp.arange(lq)[:, None] - jnp.arange(lk)[None, :]).astype(jnp.float32)
    return -slopes[:, :, None, None, None] * dist


def sink_attention(qb, kb, vb, sinks, bias, key_valid):
    s = jnp.einsum('bnqkgd,bnskd->bkgnqs', qb, kb).astype(jnp.float32) * (HEAD_DIM ** -0.5) + bias
    if key_valid is not None:
        s = jnp.where(key_valid[:, None, :], s, NEG_INF)
    sink = sinks.astype(jnp.float32).reshape(SWA_KV_HEADS, SWA_GROUP)[:, :, None, None, None]
    m = jnp.maximum(jnp.max(s, axis=-1, keepdims=True), sink)
    p = jnp.exp(s - m)
    attn = p / (jnp.sum(p, axis=-1, keepdims=True) + jnp.exp(sink - m))
    return jnp.einsum('bkgnqs,bnskd->bnqkgd', attn.astype(vb.dtype), vb)


def swa_prompt(q, k, v, sinks):
    B, T = q.shape[:2]
    nc = T // CHUNK
    span = (WINDOW_CHUNKS + 1) * CHUNK
    qb = q.reshape(B, nc, CHUNK, SWA_KV_HEADS, SWA_GROUP, HEAD_DIM)
    pad = ((0, 0), (WINDOW_CHUNKS * CHUNK, 0), (0, 0), (0, 0))
    kc = jnp.pad(k, pad).reshape(B, nc + WINDOW_CHUNKS, CHUNK, SWA_KV_HEADS, HEAD_DIM)
    vc = jnp.pad(v, pad).reshape(B, nc + WINDOW_CHUNKS, CHUNK, SWA_KV_HEADS, HEAD_DIM)
    kb = jnp.concatenate([kc[:, j:j + nc] for j in range(WINDOW_CHUNKS + 1)], axis=2)
    vb = jnp.concatenate([vc[:, j:j + nc] for j in range(WINDOW_CHUNKS + 1)], axis=2)
    key_pos = (jnp.arange(nc)[:, None] - WINDOW_CHUNKS) * CHUNK + jnp.arange(span)[None, :]
    bias = alibi_bias(WINDOW_CHUNKS * CHUNK, CHUNK, span)
    o = sink_attention(qb, kb, vb, sinks, bias, key_pos >= 0)
    return o.reshape(B, T, SWA_Q_HEADS * HEAD_DIM)


def swa_sample(q, k, v, sinks, win_k, win_v):
    B, L = q.shape[:2]
    k_all = jnp.concatenate([win_k.astype(k.dtype), k], axis=1)
    v_all = jnp.concatenate([win_v.astype(v.dtype), v], axis=1)
    qb = q.reshape(B, 1, L, SWA_KV_HEADS, SWA_GROUP, HEAD_DIM)
    bias = alibi_bias(WINDOW, L, WINDOW + L)
    o = sink_attention(qb, k_all[:, None], v_all[:, None], sinks, bias, None)
    return o.reshape(B, L, SWA_Q_HEADS * HEAD_DIM), k_all[:, -WINDOW:], v_all[:, -WINDOW:]


def token_mixers(xn, w_in, gla_gate_up_w, gla_gate_b, gla_out_norm_w, swa_sinks, w_out,
                 s0, win_k, win_v):
    B, T, _ = xn.shape
    offsets = [int(o) for o in np.cumsum(IN_WIDTHS)[:-1]]
    q_a, k_a, v_a, lr_a, g_a, q_b, k_b, v_b = jnp.split(xn @ w_in, offsets, axis=-1)
    q_a = q_a.reshape(B, T, GLA_HEADS, GLA_DK) * (GLA_DK ** -0.5)
    k_a = k_a.reshape(B, T, GLA_HEADS, GLA_DK)
    v_a = v_a.reshape(B, T, GLA_HEADS, GLA_DV)
    log_f = jax.nn.log_sigmoid((lr_a @ gla_gate_up_w + gla_gate_b).astype(jnp.float32)) / GLA_GATE_NORM
    log_f = log_f.reshape(B, T, GLA_HEADS, GLA_DK)
    if s0 is None:
        o_a, s_new = gla_recurrence(q_a, k_a, v_a, log_f,
                                    jnp.zeros((B, GLA_HEADS, GLA_DK, GLA_DV), jnp.float32), CHUNK)
    else:
        o_a, s_new = gla_recurrence(q_a, k_a, v_a, log_f, s0, T)
    o_a = rms_norm(o_a.astype(xn.dtype), gla_out_norm_w) * jax.nn.silu(g_a.reshape(B, T, GLA_HEADS, GLA_DV))
    o_a = o_a.reshape(B, T, GLA_HEADS * GLA_DV)
    q_b = q_b.reshape(B, T, SWA_Q_HEADS, HEAD_DIM)
    k_b = k_b.reshape(B, T, SWA_KV_HEADS, HEAD_DIM)
    v_b = v_b.reshape(B, T, SWA_KV_HEADS, HEAD_DIM)
    if win_k is None:
        o_b = swa_prompt(q_b, k_b, v_b, swa_sinks)
        k_keep, v_keep = k_b[:, -WINDOW:], v_b[:, -WINDOW:]
    else:
        o_b, k_keep, v_keep = swa_sample(q_b, k_b, v_b, swa_sinks, win_k, win_v)
    out = jnp.concatenate([o_a, o_b.astype(o_a.dtype)], axis=-1) @ w_out
    return out, s_new.astype(xn.dtype), k_keep, v_keep


def moe_ffn(xn, router_w, router_b, gate_w, gate_b, up_w, up_b, down_w, down_b):
    shp = xn.shape
    x2 = xn.reshape(-1, shp[-1])
    T = x2.shape[0]
    logits = x2.astype(jnp.float32) @ router_w.astype(jnp.float32) + router_b.astype(jnp.float32)
    top_val, top_idx = lax.top_k(logits, TOP_K)
    weights = jax.nn.softmax(top_val, axis=-1)
    tk = T * TOP_K
    flat_e = top_idx.reshape(tk)
    flat_tok = jnp.repeat(jnp.arange(T, dtype=jnp.int32), TOP_K)
    flat_w = weights.reshape(tk)
    order = jnp.argsort(flat_e)
    e_sorted = flat_e[order]
    counts = jnp.bincount(flat_e, length=N_EXPERTS)
    padded = (counts + MOE_BLOCK - 1) // MOE_BLOCK * MOE_BLOCK
    start = jnp.cumsum(counts) - counts
    pad_end = jnp.cumsum(padded)
    pad_start = pad_end - padded
    dest = pad_start[e_sorted] + jnp.arange(tk) - start[e_sorted]
    n_blocks = -(-(tk + N_EXPERTS * (MOE_BLOCK - 1)) // MOE_BLOCK)
    n_rows = n_blocks * MOE_BLOCK
    row_tok = jnp.full((n_rows,), T, jnp.int32).at[dest].set(flat_tok[order])
    row_w = jnp.zeros((n_rows,), jnp.float32).at[dest].set(flat_w[order])
    block_e = jnp.minimum(jnp.searchsorted(pad_end, jnp.arange(n_blocks) * MOE_BLOCK, side='right'),
                          N_EXPERTS - 1)
    x_pad = jnp.concatenate([x2, jnp.zeros((1, shp[-1]), x2.dtype)], axis=0)

    def expert_block(args):
        tok, e = args
        xb = x_pad[tok]
        g = jnp.minimum(xb @ gate_w[e] + gate_b[e], SWIGLU_LIMIT)
        u = jnp.clip(xb @ up_w[e] + up_b[e], -SWIGLU_LIMIT, SWIGLU_LIMIT)
        hmid = g * jax.nn.sigmoid(SWIGLU_ALPHA * g) * (u + 1.0)
        return hmid @ down_w[e] + down_b[e]

    y_rows = lax.map(expert_block, (row_tok.reshape(n_blocks, MOE_BLOCK), block_e))
    y = jnp.zeros((T + 1, shp[-1]), jnp.float32).at[row_tok].add(
        y_rows.reshape(n_rows, shp[-1]).astype(jnp.float32) * row_w[:, None])
    return y[:T].astype(xn.dtype).reshape(shp)


def setup_inputs(seed: int = 0) -> dict:
    key = jax.random.key(seed)
    ks = jax.random.split(key, 24)
    f32 = jnp.float32
    nrm = lambda k, shape, s: jax.random.normal(k, shape, f32) * s
    D = D_MODEL
    return {
        'x_prompt': nrm(ks[0], (BATCH, SEQ, D), 1.0),
        'x_sample': nrm(ks[1], (DEC_BATCH, DEC_SEQ, D), 1.0),
        'state_gla': nrm(ks[2], (DEPTH, DEC_BATCH, GLA_HEADS, GLA_DK, GLA_DV), 1.0),
        'cache_swa_k': nrm(ks[3], (DEPTH, DEC_BATCH, WINDOW, SWA_KV_HEADS, HEAD_DIM), 1.0),
        'cache_swa_v': nrm(ks[4], (DEPTH, DEC_BATCH, WINDOW, SWA_KV_HEADS, HEAD_DIM), 1.0),
        'attn_norm_w': 1.0 + nrm(ks[5], (DEPTH, D), 0.1),
        'w_in': nrm(ks[6], (DEPTH, D, IN_COLS), D ** -0.5),
        'gla_gate_up_w': nrm(ks[7], (DEPTH, GLA_GATE_RANK, GLA_HEADS * GLA_DK), GLA_GATE_RANK ** -0.5),
        'gla_gate_b': nrm(ks[8], (DEPTH, GLA_HEADS * GLA_DK), 0.1),
        'gla_out_norm_w': 1.0 + nrm(ks[9], (DEPTH, GLA_DV), 0.1),
        'swa_sinks': nrm(ks[10], (DEPTH, SWA_Q_HEADS), 1.0),
        'w_out': nrm(ks[11], (DEPTH, MIX_WIDTH, D), MIX_WIDTH ** -0.5),
        'ffn_norm_w': 1.0 + nrm(ks[12], (DEPTH, D), 0.1),
        'router_w': nrm(ks[13], (DEPTH, D, N_EXPERTS), D ** -0.5),
        'router_b': nrm(ks[14], (DEPTH, N_EXPERTS), 0.01),
        'expert_gate_w': nrm(ks[15], (DEPTH, N_EXPERTS, D, D_FF), D ** -0.5),
        'expert_gate_b': nrm(ks[16], (DEPTH, N_EXPERTS, D_FF), 0.01),
        'expert_up_w': nrm(ks[17], (DEPTH, N_EXPERTS, D, D_FF), D ** -0.5),
        'expert_up_b': nrm(ks[18], (DEPTH, N_EXPERTS, D_FF), 0.01),
        'expert_down_w': nrm(ks[19], (DEPTH, N_EXPERTS, D_FF, D), D_FF ** -0.5),
        'expert_down_b': nrm(ks[20], (DEPTH, N_EXPERTS, D), 0.01),
        'final_norm_w': 1.0 + nrm(ks[21], (D,), 0.1),
    }


def reference(x_prompt, x_sample, state_gla, cache_swa_k, cache_swa_v, attn_norm_w, w_in,
              gla_gate_up_w, gla_gate_b, gla_out_norm_w, swa_sinks, w_out, ffn_norm_w,
              router_w, router_b, expert_gate_w, expert_gate_b, expert_up_w, expert_up_b,
              expert_down_w, expert_down_b, final_norm_w):
    hp, hs = x_prompt, x_sample
    sp_list, kp_list, vp_list, ss_list, ks_list, vs_list = [], [], [], [], [], []
    for l in range(DEPTH):
        mix_w = (w_in[l], gla_gate_up_w[l], gla_gate_b[l], gla_out_norm_w[l], swa_sinks[l], w_out[l])
        ffn_w = (router_w[l], router_b[l], expert_gate_w[l], expert_gate_b[l], expert_up_w[l],
                 expert_up_b[l], expert_down_w[l], expert_down_b[l])
        mp, sp, kp, vp = token_mixers(rms_norm(hp, attn_norm_w[l]), *mix_w, None, None, None)
        hp = hp + mp
        hp = hp + moe_ffn(rms_norm(hp, ffn_norm_w[l]), *ffn_w)
        ms, ss, ksn, vsn = token_mixers(rms_norm(hs, attn_norm_w[l]), *mix_w,
                                        state_gla[l], cache_swa_k[l], cache_swa_v[l])
        hs = hs + ms
        hs = hs + moe_ffn(rms_norm(hs, ffn_norm_w[l]), *ffn_w)
        sp_list.append(sp); kp_list.append(kp); vp_list.append(vp)
        ss_list.append(ss); ks_list.append(ksn); vs_list.append(vsn)
    y_prompt = rms_norm(hp, final_norm_w)
    y_sample = rms_norm(hs, final_norm_w)
    state_gla_prompt = jnp.stack(sp_list)
    cache_swa_k_prompt = jnp.stack(kp_list)
    cache_swa_v_prompt = jnp.stack(vp_list)
    state_gla_sample = jnp.stack(ss_list)
    cache_swa_k_sample = jnp.stack(ks_list)
    cache_swa_v_sample = jnp.stack(vs_list)
    return (y_prompt, y_sample, state_gla_prompt, cache_swa_k_prompt, cache_swa_v_prompt,
            state_gla_sample, cache_swa_k_sample, cache_swa_v_sample)
```

```python
import functools

import jax
import jax.numpy as jnp
from jax import lax
from jax.experimental import pallas as pl
from jax.experimental.pallas import tpu as pltpu

F32 = jnp.float32
BF16 = jnp.bfloat16

CHUNK = 64
GLA_HEADS = 4
GLA_DK = 128
GLA_DV = 256
GLA_GATE_RANK = 16
GLA_GATE_NORM = 16.0
SWA_Q_HEADS = 16
SWA_KV_HEADS = 4
SWA_GROUP = SWA_Q_HEADS // SWA_KV_HEADS
HEAD_DIM = 64
WINDOW = 128
WINDOW_CHUNKS = WINDOW // CHUNK
ALIBI_MAX_BIAS = 8.0
N_EXPERTS = 32
TOP_K = 4
SWIGLU_LIMIT = 7.0
SWIGLU_ALPHA = 1.702
RMS_EPS = 1e-5
NEG_INF = -1e30

QA_W = GLA_HEADS * GLA_DK
VA_W = GLA_HEADS * GLA_DV
QB_W = SWA_Q_HEADS * HEAD_DIM
KB_W = SWA_KV_HEADS * HEAD_DIM
MAIN_W = 2 * QA_W + 2 * VA_W + QB_W + 2 * KB_W
LANE = 128
LR_PAD = LANE

PROJ_TN = 1536
ROW_TILE = 512
MOE_BM = 512
MOE_TF = 512
COMBINE_TM = 128
VMEM_LIMIT = 48 * 1024 * 1024


def _cparams(sem, vmem=VMEM_LIMIT):
    return pltpu.CompilerParams(dimension_semantics=sem, vmem_limit_bytes=vmem)


def _dot(a, b):
    return jnp.dot(a, b, preferred_element_type=F32)


def _dot_tn(a, b):
    return lax.dot_general(a, b, (((0,), (0,)), ((), ())), preferred_element_type=F32)


def _dot_nt(a, b):
    return lax.dot_general(a, b, (((1,), (1,)), ((), ())), preferred_element_type=F32)


def _in_proj_kernel(x_ref, nw_ref, w_ref, wlr_ref, out_ref, lr_ref, kv_ref, xn_ref, *, n_col_steps):
    j = pl.program_id(1)

    @pl.when(j == 0)
    def _():
        x = x_ref[...]
        y = x * lax.rsqrt(jnp.mean(x * x, axis=-1, keepdims=True) + RMS_EPS)
        xnb = (y * nw_ref[...]).astype(BF16)
        xn_ref[...] = xnb
        lr_ref[...] = _dot(xnb, wlr_ref[...]).astype(BF16)

    acc = _dot(xn_ref[...], w_ref[...])
    out_ref[...] = acc.astype(BF16)

    @pl.when(j == n_col_steps - 1)
    def _():
        kv_ref[...] = acc[:, PROJ_TN - 2 * KB_W:]


def _in_proj(x2d, norm_w, w_main, w_lr):
    n, d = x2d.shape
    tm = min(ROW_TILE, n)
    n_col_steps = MAIN_W // PROJ_TN
    return pl.pallas_call(
        functools.partial(_in_proj_kernel, n_col_steps=n_col_steps),
        out_shape=(jax.ShapeDtypeStruct((n, MAIN_W), BF16),
                   jax.ShapeDtypeStruct((n, LR_PAD), BF16),
                   jax.ShapeDtypeStruct((n, 2 * KB_W), F32)),
        grid=(n // tm, n_col_steps),
        in_specs=[pl.BlockSpec((tm, d), lambda i, j: (i, 0)),
                  pl.BlockSpec((1, d), lambda i, j: (0, 0)),
                  pl.BlockSpec((d, PROJ_TN), lambda i, j: (0, j)),
                  pl.BlockSpec((d, LR_PAD), lambda i, j: (0, 0))],
        out_specs=(pl.BlockSpec((tm, PROJ_TN), lambda i, j: (i, j)),
                   pl.BlockSpec((tm, LR_PAD), lambda i, j: (i, 0)),
                   pl.BlockSpec((tm, 2 * KB_W), lambda i, j: (i, 0))),
        scratch_shapes=[pltpu.VMEM((tm, d), BF16)],
        compiler_params=_cparams(("parallel", "arbitrary")),
        name="in_proj",
    )(x2d, norm_w, w_main, w_lr)


def _gla_kernel(*refs, chunk, n_chunks, has_s0):
    if has_s0:
        (q_ref, k_ref, v_ref, ga_ref, lr_ref, gup_ref, gb_ref, nw_ref, tri_ref, s0_ref,
         o_ref, st_ref, s_scr) = refs
    else:
        (q_ref, k_ref, v_ref, ga_ref, lr_ref, gup_ref, gb_ref, nw_ref, tri_ref,
         o_ref, st_ref, s_scr) = refs
        s0_ref = None
    t = pl.program_id(1)

    @pl.when(t == 0)
    def _():
        if has_s0:
            s_scr[...] = s0_ref[0]
        else:
            s_scr[...] = jnp.zeros_like(s_scr)

    tri = tri_ref[...]
    gup = gup_ref[...]
    gb = gb_ref[...]
    nw = nw_ref[...]

    def chunk_step(c, carry):
        r0 = pl.multiple_of(c * chunk, chunk)
        rows = pl.ds(r0, chunk)
        z = _dot(lr_ref[rows, :], gup) + gb
        fc = (jnp.minimum(z, 0.0) - jnp.log1p(jnp.exp(-jnp.abs(z)))) * (1.0 / GLA_GATE_NORM)
        hi = fc.astype(BF16)
        lo = (fc - hi.astype(F32)).astype(BF16)
        g = _dot(tri, hi) + _dot(tri, lo)
        g_end = g[chunk - 1:chunk, :]
        k_dec = (k_ref[rows, :].astype(F32) * jnp.exp(g_end - g)).astype(BF16)
        a_end = jnp.exp(g_end)
        for h in range(GLA_HEADS):
            ks = slice(h * GLA_DK, (h + 1) * GLA_DK)
            vs = slice(h * GLA_DV, (h + 1) * GLA_DV)
            upd = _dot_tn(v_ref[rows, vs], k_dec[:, ks])
            s_new = s_scr[h] * a_end[:, ks] + upd
            s_scr[h] = s_new
            o = _dot_nt(q_ref[rows, ks], s_new.astype(BF16)) * (GLA_DK ** -0.5)
            on = o * lax.rsqrt(jnp.mean(o * o, axis=-1, keepdims=True) + RMS_EPS) * nw
            gate = ga_ref[rows, vs].astype(F32)
            o_ref[rows, vs] = (on * (gate * jax.nn.sigmoid(gate))).astype(BF16)
        return carry

    lax.fori_loop(0, n_chunks, chunk_step, 0)

    @pl.when(t == pl.num_programs(1) - 1)
    def _():
        st_ref[0] = s_scr[...]


def _gla(proj, lr, gup, gate_b, norm_w, s0_t, *, batch, seq, chunk):
    rows = min(ROW_TILE, seq)
    n_chunks = rows // chunk
    nt = seq // rows
    has_s0 = s0_t is not None
    tri = (jnp.arange(chunk)[:, None] >= jnp.arange(chunk)[None, :]).astype(BF16)
    rb = lambda b, t: b * nt + t
    in_specs = [pl.BlockSpec((rows, QA_W), lambda b, t: (rb(b, t), 0)),
                pl.BlockSpec((rows, QA_W), lambda b, t: (rb(b, t), 1)),
                pl.BlockSpec((rows, VA_W), lambda b, t: (rb(b, t), 1)),
                pl.BlockSpec((rows, VA_W), lambda b, t: (rb(b, t), 2)),
                pl.BlockSpec((rows, LR_PAD), lambda b, t: (rb(b, t), 0)),
                pl.BlockSpec((LR_PAD, QA_W), lambda b, t: (0, 0)),
                pl.BlockSpec((1, QA_W), lambda b, t: (0, 0)),
                pl.BlockSpec((1, GLA_DV), lambda b, t: (0, 0)),
                pl.BlockSpec((chunk, chunk), lambda b, t: (0, 0))]
    args = [proj, proj, proj, proj, lr, gup, gate_b, norm_w, tri]
    if has_s0:
        in_specs.append(pl.BlockSpec((1, GLA_HEADS, GLA_DV, GLA_DK), lambda b, t: (b, 0, 0, 0)))
        args.append(s0_t)
    return pl.pallas_call(
        functools.partial(_gla_kernel, chunk=chunk, n_chunks=n_chunks, has_s0=has_s0),
        out_shape=(jax.ShapeDtypeStruct((batch * seq, VA_W), BF16),
                   jax.ShapeDtypeStruct((batch, GLA_HEADS, GLA_DV, GLA_DK), F32)),
        grid=(batch, nt),
        in_specs=in_specs,
        out_specs=(pl.BlockSpec((rows, VA_W), lambda b, t: (rb(b, t), 0)),
                   pl.BlockSpec((1, GLA_HEADS, GLA_DV, GLA_DK), lambda b, t: (b, 0, 0, 0))),
        scratch_shapes=[pltpu.VMEM((GLA_HEADS, GLA_DV, GLA_DK), F32)],
        compiler_params=_cparams(("parallel", "arbitrary")),
        name="gla_s0" if has_s0 else "gla",
    )(*args)


def _alibi_slopes():
    s = jnp.exp2(-ALIBI_MAX_BIAS * jnp.arange(1, SWA_Q_HEADS + 1, dtype=F32) / SWA_Q_HEADS)
    return s.reshape(SWA_KV_HEADS, SWA_GROUP)


def _stacked_bias(q_off, lq, lk):
    dist = jnp.abs(q_off + jnp.arange(lq)[:, None] - jnp.arange(lk)[None, :]).astype(F32)
    b = -_alibi_slopes()[:, :, None, None] * dist
    return b.reshape(SWA_KV_HEADS, SWA_GROUP * lq, lk)


def _stacked_sinks(sinks, lq):
    s = sinks.astype(F32).reshape(SWA_KV_HEADS, SWA_GROUP, 1)
    return jnp.broadcast_to(s, (SWA_KV_HEADS, SWA_GROUP, lq)).reshape(SWA_KV_HEADS, SWA_GROUP * lq, 1)


def _attend_group(q, keys, vals, bias, sink, valid):
    lq = q.shape[0]
    qs = jnp.concatenate([q[:, j * HEAD_DIM:(j + 1) * HEAD_DIM] for j in range(SWA_GROUP)], axis=0)
    s = _dot_nt(qs, keys) * (HEAD_DIM ** -0.5) + bias
    if valid is not None:
        s = jnp.where(valid, s, NEG_INF)
    m = jnp.maximum(jnp.max(s, axis=-1, keepdims=True), sink)
    p = jnp.exp(s - m)
    denom = jnp.sum(p, axis=-1, keepdims=True) + jnp.exp(sink - m)
    o = _dot(p.astype(BF16), vals) / denom
    return jnp.concatenate([o[j * lq:(j + 1) * lq, :] for j in range(SWA_GROUP)], axis=1)


def _swa_prompt_kernel(q_ref, kc_ref, vc_ref, kp_ref, vp_ref, bias_ref, sink_ref, o_ref,
                       k_scr, v_scr, *, n_chunks):
    t = pl.program_id(1)
    back = WINDOW_CHUNKS * CHUNK
    span = back + CHUNK
    k_scr[0:back, :] = kp_ref[...]
    v_scr[0:back, :] = vp_ref[...]
    k_scr[back:, :] = kc_ref[...]
    v_scr[back:, :] = vc_ref[...]
    key_idx = lax.broadcasted_iota(jnp.int32, (SWA_GROUP * CHUNK, span), 1)

    def chunk_step(c, carry):
        r0 = pl.multiple_of(c * CHUNK, CHUNK)
        valid = jnp.logical_or(t > 0, key_idx >= back - c * CHUNK)
        for g in range(SWA_KV_HEADS):
            hs = slice(g * HEAD_DIM, (g + 1) * HEAD_DIM)
            qs = slice(g * SWA_GROUP * HEAD_DIM, (g + 1) * SWA_GROUP * HEAD_DIM)
            o = _attend_group(q_ref[pl.ds(r0, CHUNK), qs], k_scr[pl.ds(r0, span), hs],
                              v_scr[pl.ds(r0, span), hs], bias_ref[g], sink_ref[g], valid)
            o_ref[pl.ds(r0, CHUNK), qs] = o.astype(BF16)
        return carry

    lax.fori_loop(0, n_chunks, chunk_step, 0)


def _swa_prompt(proj, sinks, *, batch, seq):
    rows = min(ROW_TILE, seq)
    n_chunks = rows // CHUNK
    nt = seq // rows
    back = WINDOW_CHUNKS * CHUNK
    pb = rows // back
    span = back + CHUNK
    bias = _stacked_bias(back, CHUNK, span)
    sink = _stacked_sinks(sinks, CHUNK)
    rb = lambda b, t: b * nt + t
    prev = lambda b, t: jnp.maximum(rb(b, t) * pb - 1, 0)
    kcol = (2 * QA_W + 2 * VA_W + QB_W) // KB_W
    return pl.pallas_call(
        functools.partial(_swa_prompt_kernel, n_chunks=n_chunks),
        out_shape=jax.ShapeDtypeStruct((batch * seq, QB_W), BF16),
        grid=(batch, nt),
        in_specs=[pl.BlockSpec((rows, QB_W), lambda b, t: (rb(b, t), (2 * QA_W + 2 * VA_W) // QB_W)),
                  pl.BlockSpec((rows, KB_W), lambda b, t: (rb(b, t), kcol)),
                  pl.BlockSpec((rows, KB_W), lambda b, t: (rb(b, t), kcol + 1)),
                  pl.BlockSpec((back, KB_W), lambda b, t: (prev(b, t), kcol)),
                  pl.BlockSpec((back, KB_W), lambda b, t: (prev(b, t), kcol + 1)),
                  pl.BlockSpec(bias.shape, lambda b, t: (0, 0, 0)),
                  pl.BlockSpec(sink.shape, lambda b, t: (0, 0, 0))],
        out_specs=pl.BlockSpec((rows, QB_W), lambda b, t: (rb(b, t), 0)),
        scratch_shapes=[pltpu.VMEM((rows + back, KB_W), BF16),
                        pltpu.VMEM((rows + back, KB_W), BF16)],
        compiler_params=_cparams(("parallel", "arbitrary")),
        name="swa_prompt",
    )(proj, proj, proj, proj, proj, bias, sink)


def _swa_sample_kernel(q_ref, kn_ref, vn_ref, wk_ref, wv_ref, bias_ref, sink_ref, o_ref):
    keys = jnp.concatenate([wk_ref[0].astype(BF16), kn_ref[...]], axis=0)
    vals = jnp.concatenate([wv_ref[0].astype(BF16), vn_ref[...]], axis=0)
    for g in range(SWA_KV_HEADS):
        hs = slice(g * HEAD_DIM, (g + 1) * HEAD_DIM)
        qs = slice(g * SWA_GROUP * HEAD_DIM, (g + 1) * SWA_GROUP * HEAD_DIM)
        o = _attend_group(q_ref[:, qs], keys[:, hs], vals[:, hs], bias_ref[g], sink_ref[g], None)
        o_ref[:, qs] = o.astype(BF16)


def _swa_sample(proj, sinks, win_k, win_v, *, batch, seq):
    bias = _stacked_bias(WINDOW, seq, WINDOW + seq)
    sink = _stacked_sinks(sinks, seq)
    kcol = (2 * QA_W + 2 * VA_W + QB_W) // KB_W
    return pl.pallas_call(
        _swa_sample_kernel,
        out_shape=jax.ShapeDtypeStruct((batch * seq, QB_W), BF16),
        grid=(batch,),
        in_specs=[pl.BlockSpec((seq, QB_W), lambda b: (b, (2 * QA_W + 2 * VA_W) // QB_W)),
                  pl.BlockSpec((seq, KB_W), lambda b: (b, kcol)),
                  pl.BlockSpec((seq, KB_W), lambda b: (b, kcol + 1)),
                  pl.BlockSpec((1, WINDOW, KB_W), lambda b: (b, 0, 0)),
                  pl.BlockSpec((1, WINDOW, KB_W), lambda b: (b, 0, 0)),
                  pl.BlockSpec(bias.shape, lambda b: (0, 0, 0)),
                  pl.BlockSpec(sink.shape, lambda b: (0, 0, 0))],
        out_specs=pl.BlockSpec((seq, QB_W), lambda b: (b, 0)),
        compiler_params=_cparams(("parallel",)),
        name="swa_sample",
    )(proj, proj, proj, win_k, win_v, bias, sink)


def _out_proj_kernel(oa_ref, ob_ref, w_ref, x_ref, nw_ref, rw_ref, rb_ref, *rest, aliased):
    if aliased:
        _, _, h_ref, xn_ref, lg_ref = rest
    else:
        h_ref, xn_ref, lg_ref = rest
    mix = _dot(oa_ref[...], w_ref[0:VA_W, :]) + _dot(ob_ref[...], w_ref[VA_W:, :])
    h = x_ref[...] + mix
    h_ref[...] = h
    y = h * lax.rsqrt(jnp.mean(h * h, axis=-1, keepdims=True) + RMS_EPS)
    xn = y * nw_ref[...]
    xn_ref[...] = xn
    lg_ref[...] = jnp.dot(xn, rw_ref[...], preferred_element_type=F32,
                          precision=lax.Precision.HIGHEST) + rb_ref[...]


def _out_proj(o_a, o_b, w_out, x2d, norm_w, router_w, router_b, *, n_total, row_off, prev=None):
    n, d = x2d.shape
    tm = min(ROW_TILE, n)
    ob = row_off // tm
    aliased = prev is not None
    in_specs = [pl.BlockSpec((tm, VA_W), lambda i: (i, 0)),
                pl.BlockSpec((tm, QB_W), lambda i: (i, 0)),
                pl.BlockSpec((VA_W + QB_W, d), lambda i: (0, 0)),
                pl.BlockSpec((tm, d), lambda i: (i, 0)),
                pl.BlockSpec((1, d), lambda i: (0, 0)),
                pl.BlockSpec((d, N_EXPERTS), lambda i: (0, 0)),
                pl.BlockSpec((1, N_EXPERTS), lambda i: (0, 0))]
    args = [o_a, o_b, w_out, x2d, norm_w, router_w, router_b]
    io_alias = {}
    if aliased:
        in_specs += [pl.BlockSpec(memory_space=pl.ANY), pl.BlockSpec(memory_space=pl.ANY)]
        args += list(prev)
        io_alias = {7: 1, 8: 2}
    return pl.pallas_call(
        functools.partial(_out_proj_kernel, aliased=aliased),
        out_shape=(jax.ShapeDtypeStruct((n, d), F32),
                   jax.ShapeDtypeStruct((n_total, d), F32),
                   jax.ShapeDtypeStruct((n_total, N_EXPERTS), F32)),
        grid=(n // tm,),
        in_specs=in_specs,
        out_specs=(pl.BlockSpec((tm, d), lambda i: (i, 0)),
                   pl.BlockSpec((tm, d), lambda i: (i + ob, 0)),
                   pl.BlockSpec((tm, N_EXPERTS), lambda i: (i + ob, 0))),
        input_output_aliases=io_alias,
        compiler_params=_cparams(("parallel",)),
        name="out_proj",
    )(*args)


def _route(logits, bm):
    n = logits.shape[0]
    tk = n * TOP_K
    top_val, top_idx = lax.top_k(logits, TOP_K)
    weights = jax.nn.softmax(top_val, axis=-1)
    flat_e = top_idx.reshape(tk)
    onehot = (flat_e[:, None] == jnp.arange(N_EXPERTS, dtype=flat_e.dtype)[None, :]).astype(jnp.int32)
    csum = jnp.cumsum(onehot, axis=0)
    rank = jnp.sum((csum - onehot) * onehot, axis=1)
    counts = csum[-1]
    padded = (counts + bm - 1) // bm * bm
    pad_end = jnp.cumsum(padded)
    pad_start = pad_end - padded
    dest = jnp.sum(onehot * pad_start[None, :], axis=1) + rank
    n_blocks = -(-(tk + N_EXPERTS * (bm - 1)) // bm)
    n_rows = n_blocks * bm
    row_tok = jnp.zeros((n_rows,), jnp.int32).at[dest].set(jnp.arange(tk, dtype=jnp.int32) // TOP_K)
    block_e = jnp.minimum(jnp.searchsorted(pad_end, jnp.arange(n_blocks) * bm, side='right'),
                          N_EXPERTS - 1).astype(jnp.int32)
    n_real = (pad_end[-1] // bm).astype(jnp.int32).reshape(1)
    return weights, dest.astype(jnp.int32), row_tok, block_e, n_real, n_blocks


def _row_copy(src_ref, dst_ref, sem, src_row, dst_row, n_rows=1):
    return pltpu.make_async_copy(src_ref.at[pl.ds(src_row, n_rows)],
                                 dst_ref.at[pl.ds(dst_row, n_rows)], sem)


def _gather_kernel(nreal_ref, idx_ref, src_ref, dst_ref, sem, *, bm):
    blk = pl.program_id(0)

    @pl.when(blk < nreal_ref[0])
    def _():
        base = blk * bm

        def issue(i, carry):
            _row_copy(src_ref, dst_ref, sem, idx_ref[0, 0, i], base + i).start()
            return carry

        lax.fori_loop(0, bm, issue, 0)
        _row_copy(src_ref, dst_ref, sem, 0, base, bm).wait()


def _gather_rows(src, row_tok, n_real, *, bm):
    n_rows = row_tok.shape[0]
    n_blocks = n_rows // bm
    return pl.pallas_call(
        functools.partial(_gather_kernel, bm=bm),
        out_shape=jax.ShapeDtypeStruct((n_rows, src.shape[1]), src.dtype),
        grid_spec=pltpu.PrefetchScalarGridSpec(
            num_scalar_prefetch=1,
            grid=(n_blocks,),
            in_specs=[pl.BlockSpec((1, 1, bm), lambda i, nr: (i, 0, 0), memory_space=pltpu.SMEM),
                      pl.BlockSpec(memory_space=pl.ANY)],
            out_specs=pl.BlockSpec(memory_space=pl.ANY),
            scratch_shapes=[pltpu.SemaphoreType.DMA]),
        compiler_params=_cparams(("arbitrary",)),
        name="moe_gather",
    )(n_real, row_tok.reshape(n_blocks, 1, bm), src)


def _moe_kernel(be_ref, nreal_ref, x_ref, gw_ref, gb_ref, uw_ref, ub_ref, dw_ref, db_ref,
                y_ref, acc_ref, xb_ref):
    blk = pl.program_id(0)
    f = pl.program_id(1)
    last = pl.num_programs(1) - 1

    @pl.when(blk < nreal_ref[0])
    def _():
        @pl.when(f == 0)
        def _():
            xb_ref[...] = x_ref[...].astype(BF16)

        xb = xb_ref[...]
        g = jnp.minimum(_dot(xb, gw_ref[0]) + gb_ref[0], SWIGLU_LIMIT)
        u = jnp.clip(_dot(xb, uw_ref[0]) + ub_ref[0], -SWIGLU_LIMIT, SWIGLU_LIMIT)
        hmid = g * jax.nn.sigmoid(SWIGLU_ALPHA * g) * (u + 1.0)
        part = _dot(hmid.astype(BF16), dw_ref[0])

        @pl.when(f == 0)
        def _():
            acc_ref[...] = part

        @pl.when(f > 0)
        def _():
            acc_ref[...] += part

        @pl.when(f == last)
        def _():
            y_ref[...] = acc_ref[...] + db_ref[0]


def _moe_experts(x_rows, block_e, n_real, gate_w, gate_b, up_w, up_b, down_w, down_b, *, bm):
    n_rows, d = x_rows.shape
    n_blocks = n_rows // bm
    d_ff = gate_w.shape[-1]
    tf = min(MOE_TF, d_ff)
    nf = d_ff // tf

    def blk_of(b, nr):
        return jnp.minimum(b, nr[0] - 1)

    def f_of(b, f, nr):
        return jnp.where(b < nr[0], f, nf - 1)

    return pl.pallas_call(
        _moe_kernel,
        out_shape=jax.ShapeDtypeStruct((n_rows, d), F32),
        grid_spec=pltpu.PrefetchScalarGridSpec(
            num_scalar_prefetch=2,
            grid=(n_blocks, nf),
            in_specs=[
                pl.BlockSpec((bm, d), lambda b, f, be, nr: (blk_of(b, nr), 0)),
                pl.BlockSpec((1, d, tf), lambda b, f, be, nr: (be[blk_of(b, nr)], 0, f_of(b, f, nr))),
                pl.BlockSpec((1, 1, tf), lambda b, f, be, nr: (be[blk_of(b, nr)], 0, f_of(b, f, nr))),
                pl.BlockSpec((1, d, tf), lambda b, f, be, nr: (be[blk_of(b, nr)], 0, f_of(b, f, nr))),
                pl.BlockSpec((1, 1, tf), lambda b, f, be, nr: (be[blk_of(b, nr)], 0, f_of(b, f, nr))),
                pl.BlockSpec((1, tf, d), lambda b, f, be, nr: (be[blk_of(b, nr)], f_of(b, f, nr), 0)),
                pl.BlockSpec((1, 1, d), lambda b, f, be, nr: (be[blk_of(b, nr)], 0, 0)),
            ],
            out_specs=pl.BlockSpec((bm, d), lambda b, f, be, nr: (blk_of(b, nr), 0)),
            scratch_shapes=[pltpu.VMEM((bm, d), F32), pltpu.VMEM((bm, d), BF16)]),
        compiler_params=_cparams(("arbitrary", "arbitrary")),
        name="moe_experts",
    )(block_e, n_real, x_rows, gate_w, gate_b, up_w, up_b, down_w, down_b)


def _combine_kernel(idx_ref, w_ref, h_ref, nw_ref, rows_ref, out_ref, buf, sem, *, tm):
    def issue(i, carry):
        for k in range(TOP_K):
            _row_copy(rows_ref, buf.at[k], sem, idx_ref[0, 0, i * TOP_K + k], i).start()
        return carry

    lax.fori_loop(0, tm, issue, 0)
    for k in range(TOP_K):
        _row_copy(rows_ref, buf.at[k], sem, 0, 0, tm).wait()

    w = w_ref[...]
    y = w[:, 0:1] * buf[0]
    for k in range(1, TOP_K):
        y = y + w[:, k:k + 1] * buf[k]
    h = h_ref[...] + y
    out = h * lax.rsqrt(jnp.mean(h * h, axis=-1, keepdims=True) + RMS_EPS)
    out_ref[...] = out * nw_ref[...]


def _combine(y_rows, dest, weights, h, final_norm_w, *, tok_off):
    n, d = h.shape
    tm = min(COMBINE_TM, n)
    nb = dest.shape[0] // (tm * TOP_K)
    ob = tok_off // tm
    return pl.pallas_call(
        functools.partial(_combine_kernel, tm=tm),
        out_shape=jax.ShapeDtypeStruct((n, d), F32),
        grid=(n // tm,),
        in_specs=[pl.BlockSpec((1, 1, tm * TOP_K), lambda i: (i + ob, 0, 0), memory_space=pltpu.SMEM),
                  pl.BlockSpec((tm, TOP_K), lambda i: (i + ob, 0)),
                  pl.BlockSpec((tm, d), lambda i: (i, 0)),
                  pl.BlockSpec((1, d), lambda i: (0, 0)),
                  pl.BlockSpec(memory_space=pl.ANY)],
        out_specs=pl.BlockSpec((tm, d), lambda i: (i, 0)),
        scratch_shapes=[pltpu.VMEM((TOP_K, tm, d), F32), pltpu.SemaphoreType.DMA],
        compiler_params=_cparams(("arbitrary",)),
        name="moe_combine",
    )(dest.reshape(nb, 1, tm * TOP_K), weights, h, final_norm_w, y_rows)


def _split_w_in(w_in):
    lr0 = 2 * QA_W + VA_W
    w_main = jnp.concatenate([w_in[:, :lr0], w_in[:, lr0 + GLA_GATE_RANK:]], axis=1).astype(BF16)
    w_lr = jnp.pad(w_in[:, lr0:lr0 + GLA_GATE_RANK], ((0, 0), (0, LR_PAD - GLA_GATE_RANK))).astype(BF16)
    return w_main, w_lr


def kernel(x_prompt, x_sample, state_gla, cache_swa_k, cache_swa_v, attn_norm_w, w_in, gla_gate_up_w, gla_gate_b, gla_out_norm_w, swa_sinks, w_out, ffn_norm_w, router_w, router_b, expert_gate_w, expert_gate_b, expert_up_w, expert_up_b, expert_down_w, expert_down_b, final_norm_w):
    depth = w_in.shape[0]
    bp, tp, d = x_prompt.shape
    bs, ts, _ = x_sample.shape
    n_p, n_s = bp * tp, bs * ts
    n_all = n_p + n_s
    hp = x_prompt.reshape(n_p, d)
    hs = x_sample.reshape(n_s, d)
    outs = [[] for _ in range(6)]
    for l in range(depth):
        w_main, w_lr = _split_w_in(w_in[l])
        gup = jnp.pad(gla_gate_up_w[l], ((0, LR_PAD - GLA_GATE_RANK), (0, 0))).astype(BF16)
        gate_b = gla_gate_b[l].reshape(1, QA_W)
        onw = gla_out_norm_w[l].reshape(1, GLA_DV)
        anw = attn_norm_w[l].reshape(1, d)
        fnw = ffn_norm_w[l].reshape(1, d)
        w_o = w_out[l].astype(BF16)
        r_w = router_w[l]
        r_b = router_b[l].reshape(1, N_EXPERTS)

        proj_p, lr_p, kv_p = _in_proj(hp, anw, w_main, w_lr)
        oa_p, st_p = _gla(proj_p, lr_p, gup, gate_b, onw, None, batch=bp, seq=tp, chunk=CHUNK)
        ob_p = _swa_prompt(proj_p, swa_sinks[l], batch=bp, seq=tp)
        proj_s, lr_s, kv_s = _in_proj(hs, anw, w_main, w_lr)
        s0_t = jnp.swapaxes(state_gla[l], -1, -2)
        oa_s, st_s = _gla(proj_s, lr_s, gup, gate_b, onw, s0_t, batch=bs, seq=ts, chunk=ts)
        win_k = cache_swa_k[l].reshape(bs, WINDOW, KB_W)
        win_v = cache_swa_v[l].reshape(bs, WINDOW, KB_W)
        ob_s = _swa_sample(proj_s, swa_sinks[l], win_k, win_v, batch=bs, seq=ts)

        hp, xn_all, lg_all = _out_proj(oa_p, ob_p, w_o, hp, fnw, r_w, r_b, n_total=n_all, row_off=0)
        hs, xn_all, lg_all = _out_proj(oa_s, ob_s, w_o, hs, fnw, r_w, r_b, n_total=n_all, row_off=n_p,
                                       prev=(xn_all, lg_all))

        weights, dest, row_tok, block_e, n_real, _ = _route(lg_all, MOE_BM)
        x_rows = _gather_rows(xn_all, row_tok, n_real, bm=MOE_BM)
        y_rows = _moe_experts(x_rows, block_e, n_real,
                              expert_gate_w[l].astype(BF16), expert_gate_b[l][:, None, :],
                              expert_up_w[l].astype(BF16), expert_up_b[l][:, None, :],
                              expert_down_w[l].astype(BF16), expert_down_b[l][:, None, :], bm=MOE_BM)
        last = l == depth - 1
        fin = final_norm_w.reshape(1, d) if last else None
        assert last, "intermediate layers would need a combine without the final norm"
        hp = _combine(y_rows, dest, weights, hp, fin, tok_off=0)
        hs = _combine(y_rows, dest, weights, hs, fin, tok_off=n_p)

        outs[0].append(jnp.swapaxes(st_p, -1, -2))
        kvp = kv_p.reshape(bp, tp, 2, SWA_KV_HEADS, HEAD_DIM)[:, -WINDOW:]
        outs[1].append(kvp[:, :, 0])
        outs[2].append(kvp[:, :, 1])
        outs[3].append(jnp.swapaxes(st_s, -1, -2))
        kvs = kv_s.reshape(bs, ts, 2, SWA_KV_HEADS, HEAD_DIM)
        outs[4].append(jnp.concatenate([cache_swa_k[l], kvs[:, :, 0]], axis=1)[:, -WINDOW:])
        outs[5].append(jnp.concatenate([cache_swa_v[l], kvs[:, :, 1]], axis=1)[:, -WINDOW:])

    y_prompt = hp.reshape(bp, tp, d)
    y_sample = hs.reshape(bs, ts, d)
    return (y_prompt, y_sample) + tuple(jnp.stack(o) for o in outs)
```

```python
import functools

import jax
import jax.numpy as jnp
from jax import lax
from jax.experimental import pallas as pl
from jax.experimental.pallas import tpu as pltpu

F32 = jnp.float32
BF16 = jnp.bfloat16

CHUNK = 64
GLA_HEADS = 4
GLA_DK = 128
GLA_DV = 256
GLA_GATE_RANK = 16
GLA_GATE_NORM = 16.0
SWA_Q_HEADS = 16
SWA_KV_HEADS = 4
SWA_GROUP = SWA_Q_HEADS // SWA_KV_HEADS
HEAD_DIM = 64
WINDOW = 128
WINDOW_CHUNKS = WINDOW // CHUNK
ALIBI_MAX_BIAS = 8.0
N_EXPERTS = 32
TOP_K = 4
SWIGLU_LIMIT = 7.0
SWIGLU_ALPHA = 1.702
RMS_EPS = 1e-5
NEG_INF = -1e30

QA_W = GLA_HEADS * GLA_DK
VA_W = GLA_HEADS * GLA_DV
QB_W = SWA_Q_HEADS * HEAD_DIM
KB_W = SWA_KV_HEADS * HEAD_DIM
MIX_W = VA_W + QB_W
MAIN_W = 2 * QA_W + 2 * VA_W + QB_W + 2 * KB_W
QB_COL = (2 * QA_W + 2 * VA_W) // QB_W
KB_COL = (2 * QA_W + 2 * VA_W + QB_W) // KB_W
LANE = 128
LR_PAD = LANE

PROJ_TN = 1536
ROW_TILE = 512
MOE_BM = 512
MOE_TF = 512
OUT_TM = 256
COMBINE_TM = 128
VMEM_LIMIT = 48 * 1024 * 1024


def _cparams(sem, vmem=VMEM_LIMIT):
    return pltpu.CompilerParams(dimension_semantics=sem, vmem_limit_bytes=vmem)


def _dot(a, b):
    return jnp.dot(a, b, preferred_element_type=F32)


def _dot_tn(a, b):
    return lax.dot_general(a, b, (((0,), (0,)), ((), ())), preferred_element_type=F32)


def _dot_nt(a, b):
    return lax.dot_general(a, b, (((1,), (1,)), ((), ())), preferred_element_type=F32)


def _rms(x):
    return x * lax.rsqrt(jnp.mean(x * x, axis=-1, keepdims=True) + RMS_EPS)


def _hi_lo(x):
    hi = x.astype(BF16)
    return hi, (x - hi.astype(F32)).astype(BF16)


def _in_proj_kernel(x_ref, nw_ref, w_ref, wlr_ref, out_ref, lr_ref, kv_ref, xn_ref, *, n_col_steps):
    j = pl.program_id(1)

    @pl.when(j == 0)
    def _():
        xnb = (_rms(x_ref[...]) * nw_ref[...]).astype(BF16)
        xn_ref[...] = xnb
        lr_ref[...] = _dot(xnb, wlr_ref[...]).astype(BF16)

    acc = _dot(xn_ref[...], w_ref[...])
    out_ref[...] = acc.astype(BF16)

    @pl.when(j == n_col_steps - 1)
    def _():
        kv_ref[...] = acc[:, PROJ_TN - 2 * KB_W:]


def _in_proj(x2d, norm_w, w_main, w_lr):
    n, d = x2d.shape
    tm = min(ROW_TILE, n)
    n_col_steps = MAIN_W // PROJ_TN
    return pl.pallas_call(
        functools.partial(_in_proj_kernel, n_col_steps=n_col_steps),
        out_shape=(jax.ShapeDtypeStruct((n, MAIN_W), BF16),
                   jax.ShapeDtypeStruct((n, LR_PAD), BF16),
                   jax.ShapeDtypeStruct((n, 2 * KB_W), F32)),
        grid=(n // tm, n_col_steps),
        in_specs=[pl.BlockSpec((tm, d), lambda i, j: (i, 0)),
                  pl.BlockSpec((1, d), lambda i, j: (0, 0)),
                  pl.BlockSpec((d, PROJ_TN), lambda i, j: (0, j)),
                  pl.BlockSpec((d, LR_PAD), lambda i, j: (0, 0))],
        out_specs=(pl.BlockSpec((tm, PROJ_TN), lambda i, j: (i, j)),
                   pl.BlockSpec((tm, LR_PAD), lambda i, j: (i, 0)),
                   pl.BlockSpec((tm, 2 * KB_W), lambda i, j: (i, 0))),
        scratch_shapes=[pltpu.VMEM((tm, d), BF16)],
        compiler_params=_cparams(("parallel", "arbitrary")),
        name="in_proj",
    )(x2d, norm_w, w_main, w_lr)


def _gla_chunk(rows, chunk, q_ref, k_ref, v_ref, ga_ref, lr_ref, gup, gb, nw, tri, s_scr, o_ref):
    z = _dot(lr_ref[rows, :], gup) + gb
    fc = (jnp.minimum(z, 0.0) - jnp.log1p(jnp.exp(-jnp.abs(z)))) * (1.0 / GLA_GATE_NORM)
    hi, lo = _hi_lo(fc)
    g = _dot(tri, hi) + _dot(tri, lo)
    g_end = g[chunk - 1:chunk, :]
    k_dec = (k_ref[rows, :].astype(F32) * jnp.exp(g_end - g)).astype(BF16)
    a_end = jnp.exp(g_end)
    for h in range(GLA_HEADS):
        ks = slice(h * GLA_DK, (h + 1) * GLA_DK)
        vs = slice(h * GLA_DV, (h + 1) * GLA_DV)
        upd = _dot_tn(v_ref[rows, vs], k_dec[:, ks])
        s_new = s_scr[h] * a_end[:, ks] + upd
        s_scr[h] = s_new
        o = _dot_nt(q_ref[rows, ks], s_new.astype(BF16)) * (GLA_DK ** -0.5)
        gate = ga_ref[rows, vs].astype(F32)
        o_ref[rows, vs] = (_rms(o) * nw * (gate * jax.nn.sigmoid(gate))).astype(BF16)


def _alibi_slopes():
    s = jnp.exp2(-ALIBI_MAX_BIAS * jnp.arange(1, SWA_Q_HEADS + 1, dtype=F32) / SWA_Q_HEADS)
    return s.reshape(SWA_KV_HEADS, SWA_GROUP)


def _stacked_bias(q_off, lq, lk):
    dist = jnp.abs(q_off + jnp.arange(lq)[:, None] - jnp.arange(lk)[None, :]).astype(F32)
    b = -_alibi_slopes()[:, :, None, None] * dist
    return b.reshape(SWA_KV_HEADS, SWA_GROUP * lq, lk)


def _stacked_sinks(sinks, lq):
    s = sinks.astype(F32).reshape(SWA_KV_HEADS, SWA_GROUP, 1)
    return jnp.broadcast_to(s, (SWA_KV_HEADS, SWA_GROUP, lq)).reshape(SWA_KV_HEADS, SWA_GROUP * lq, 1)


def _attend_group(q, keys, vals, bias, sink, valid):
    lq = q.shape[0]
    qs = jnp.concatenate([q[:, j * HEAD_DIM:(j + 1) * HEAD_DIM] for j in range(SWA_GROUP)], axis=0)
    s = _dot_nt(qs, keys) * (HEAD_DIM ** -0.5) + bias
    if valid is not None:
        s = jnp.where(valid, s, NEG_INF)
    m = jnp.maximum(jnp.max(s, axis=-1, keepdims=True), sink)
    p = jnp.exp(s - m)
    denom = jnp.sum(p, axis=-1, keepdims=True) + jnp.exp(sink - m)
    o = _dot(p.astype(BF16), vals) / denom
    return jnp.concatenate([o[j * lq:(j + 1) * lq, :] for j in range(SWA_GROUP)], axis=1)


def _mix_prompt_kernel(q_ref, k_ref, v_ref, ga_ref, lr_ref, qb_ref, kc_ref, vc_ref, kp_ref, vp_ref,
                       gup_ref, gb_ref, nw_ref, tri_ref, bias_ref, sink_ref,
                       o_ref, st_ref, s_scr, k_scr, v_scr, *, n_chunks):
    t = pl.program_id(1)
    back = WINDOW_CHUNKS * CHUNK
    span = back + CHUNK

    @pl.when(t == 0)
    def _():
        s_scr[...] = jnp.zeros_like(s_scr)

    k_scr[0:back, :] = kp_ref[...]
    v_scr[0:back, :] = vp_ref[...]
    k_scr[back:, :] = kc_ref[...]
    v_scr[back:, :] = vc_ref[...]
    key_idx = lax.broadcasted_iota(jnp.int32, (SWA_GROUP * CHUNK, span), 1)
    gup, gb, nw, tri = gup_ref[...], gb_ref[...], nw_ref[...], tri_ref[...]

    def chunk_step(c, carry):
        r0 = pl.multiple_of(c * CHUNK, CHUNK)
        rows = pl.ds(r0, CHUNK)
        _gla_chunk(rows, CHUNK, q_ref, k_ref, v_ref, ga_ref, lr_ref, gup, gb, nw, tri, s_scr, o_ref)
        valid = jnp.logical_or(t > 0, key_idx >= back - c * CHUNK)
        for g in range(SWA_KV_HEADS):
            hs = slice(g * HEAD_DIM, (g + 1) * HEAD_DIM)
            qs = slice(g * SWA_GROUP * HEAD_DIM, (g + 1) * SWA_GROUP * HEAD_DIM)
            o = _attend_group(qb_ref[rows, qs], k_scr[pl.ds(r0, span), hs],
                              v_scr[pl.ds(r0, span), hs], bias_ref[g], sink_ref[g], valid)
            o_ref[rows, VA_W + g * SWA_GROUP * HEAD_DIM:VA_W + (g + 1) * SWA_GROUP * HEAD_DIM] = o.astype(BF16)
        return carry

    lax.fori_loop(0, n_chunks, chunk_step, 0, unroll=2)

    @pl.when(t == pl.num_programs(1) - 1)
    def _():
        st_ref[0] = s_scr[...]


def _gla_consts(gup, gate_b, norm_w, chunk):
    tri = (jnp.arange(chunk)[:, None] >= jnp.arange(chunk)[None, :]).astype(BF16)
    specs = [pl.BlockSpec((LR_PAD, QA_W), lambda *_: (0, 0)),
             pl.BlockSpec((1, QA_W), lambda *_: (0, 0)),
             pl.BlockSpec((1, GLA_DV), lambda *_: (0, 0)),
             pl.BlockSpec((chunk, chunk), lambda *_: (0, 0))]
    return specs, [gup, gate_b, norm_w, tri]


def _mix_prompt(proj, lr, gup, gate_b, norm_w, sinks, *, batch, seq):
    rows = min(ROW_TILE, seq)
    n_chunks = rows // CHUNK
    nt = seq // rows
    back = WINDOW_CHUNKS * CHUNK
    pb = rows // back
    bias = _stacked_bias(back, CHUNK, back + CHUNK)
    sink = _stacked_sinks(sinks, CHUNK)
    rb = lambda b, t: b * nt + t
    prev = lambda b, t: jnp.maximum(rb(b, t) * pb - 1, 0)
    cspecs, cargs = _gla_consts(gup, gate_b, norm_w, CHUNK)
    return pl.pallas_call(
        functools.partial(_mix_prompt_kernel, n_chunks=n_chunks),
        out_shape=(jax.ShapeDtypeStruct((batch * seq, MIX_W), BF16),
                   jax.ShapeDtypeStruct((batch, GLA_HEADS, GLA_DV, GLA_DK), F32)),
        grid=(batch, nt),
        in_specs=[pl.BlockSpec((rows, QA_W), lambda b, t: (rb(b, t), 0)),
                  pl.BlockSpec((rows, QA_W), lambda b, t: (rb(b, t), 1)),
                  pl.BlockSpec((rows, VA_W), lambda b, t: (rb(b, t), 1)),
                  pl.BlockSpec((rows, VA_W), lambda b, t: (rb(b, t), 2)),
                  pl.BlockSpec((rows, LR_PAD), lambda b, t: (rb(b, t), 0)),
                  pl.BlockSpec((rows, QB_W), lambda b, t: (rb(b, t), QB_COL)),
                  pl.BlockSpec((rows, KB_W), lambda b, t: (rb(b, t), KB_COL)),
                  pl.BlockSpec((rows, KB_W), lambda b, t: (rb(b, t), KB_COL + 1)),
                  pl.BlockSpec((back, KB_W), lambda b, t: (prev(b, t), KB_COL)),
                  pl.BlockSpec((back, KB_W), lambda b, t: (prev(b, t), KB_COL + 1)),
                  *cspecs,
                  pl.BlockSpec(bias.shape, lambda b, t: (0, 0, 0)),
                  pl.BlockSpec(sink.shape, lambda b, t: (0, 0, 0))],
        out_specs=(pl.BlockSpec((rows, MIX_W), lambda b, t: (rb(b, t), 0)),
                   pl.BlockSpec((1, GLA_HEADS, GLA_DV, GLA_DK), lambda b, t: (b, 0, 0, 0))),
        scratch_shapes=[pltpu.VMEM((GLA_HEADS, GLA_DV, GLA_DK), F32),
                        pltpu.VMEM((rows + back, KB_W), BF16),
                        pltpu.VMEM((rows + back, KB_W), BF16)],
        compiler_params=_cparams(("parallel", "arbitrary")),
        name="mix_prompt",
    )(proj, proj, proj, proj, lr, proj, proj, proj, proj, proj, *cargs, bias, sink)


def _mix_sample_kernel(q_ref, k_ref, v_ref, ga_ref, lr_ref, qb_ref, kn_ref, vn_ref, wk_ref, wv_ref,
                       s0_ref, gup_ref, gb_ref, nw_ref, tri_ref, bias_ref, sink_ref,
                       o_ref, st_ref, s_scr, *, seq):
    s_scr[...] = s0_ref[0]
    _gla_chunk(slice(0, seq), seq, q_ref, k_ref, v_ref, ga_ref, lr_ref,
               gup_ref[...], gb_ref[...], nw_ref[...], tri_ref[...], s_scr, o_ref)
    st_ref[0] = s_scr[...]
    keys = jnp.concatenate([wk_ref[0].astype(BF16), kn_ref[...]], axis=0)
    vals = jnp.concatenate([wv_ref[0].astype(BF16), vn_ref[...]], axis=0)
    for g in range(SWA_KV_HEADS):
        hs = slice(g * HEAD_DIM, (g + 1) * HEAD_DIM)
        qs = slice(g * SWA_GROUP * HEAD_DIM, (g + 1) * SWA_GROUP * HEAD_DIM)
        o = _attend_group(qb_ref[:, qs], keys[:, hs], vals[:, hs], bias_ref[g], sink_ref[g], None)
        o_ref[:, VA_W + g * SWA_GROUP * HEAD_DIM:VA_W + (g + 1) * SWA_GROUP * HEAD_DIM] = o.astype(BF16)


def _mix_sample(proj, lr, gup, gate_b, norm_w, sinks, s0_t, win_k, win_v, *, batch, seq):
    bias = _stacked_bias(WINDOW, seq, WINDOW + seq)
    sink = _stacked_sinks(sinks, seq)
    cspecs, cargs = _gla_consts(gup, gate_b, norm_w, seq)
    state_spec = pl.BlockSpec((1, GLA_HEADS, GLA_DV, GLA_DK), lambda b: (b, 0, 0, 0))
    return pl.pallas_call(
        functools.partial(_mix_sample_kernel, seq=seq),
        out_shape=(jax.ShapeDtypeStruct((batch * seq, MIX_W), BF16),
                   jax.ShapeDtypeStruct((batch, GLA_HEADS, GLA_DV, GLA_DK), F32)),
        grid=(batch,),
        in_specs=[pl.BlockSpec((seq, QA_W), lambda b: (b, 0)),
                  pl.BlockSpec((seq, QA_W), lambda b: (b, 1)),
                  pl.BlockSpec((seq, VA_W), lambda b: (b, 1)),
                  pl.BlockSpec((seq, VA_W), lambda b: (b, 2)),
                  pl.BlockSpec((seq, LR_PAD), lambda b: (b, 0)),
                  pl.BlockSpec((seq, QB_W), lambda b: (b, QB_COL)),
                  pl.BlockSpec((seq, KB_W), lambda b: (b, KB_COL)),
                  pl.BlockSpec((seq, KB_W), lambda b: (b, KB_COL + 1)),
                  pl.BlockSpec((1, WINDOW, KB_W), lambda b: (b, 0, 0)),
                  pl.BlockSpec((1, WINDOW, KB_W), lambda b: (b, 0, 0)),
                  state_spec,
                  *cspecs,
                  pl.BlockSpec(bias.shape, lambda b: (0, 0, 0)),
                  pl.BlockSpec(sink.shape, lambda b: (0, 0, 0))],
        out_specs=(pl.BlockSpec((seq, MIX_W), lambda b: (b, 0)), state_spec),
        scratch_shapes=[pltpu.VMEM((GLA_HEADS, GLA_DV, GLA_DK), F32)],
        compiler_params=_cparams(("parallel",)),
        name="mix_sample",
    )(proj, proj, proj, proj, lr, proj, proj, proj, win_k, win_v, s0_t, *cargs, bias, sink)


def _out_proj_kernel(mp_ref, xp_ref, ms_ref, xs_ref, w_ref, nw_ref, rwh_ref, rwl_ref, rb_ref,
                     hp_ref, hs_ref, xn_ref, lg_ref, *, p_tiles):
    i = pl.program_id(0)

    def tile(m_ref, x_ref, h_ref):
        h = x_ref[...] + _dot(m_ref[...], w_ref[...])
        h_ref[...] = h
        xn = _rms(h) * nw_ref[...]
        xn_ref[...] = xn
        xh, xl = _hi_lo(xn)
        rwh = rwh_ref[...]
        lg_ref[...] = _dot(xh, rwh) + _dot(xl, rwh) + _dot(xh, rwl_ref[...]) + rb_ref[...]

    @pl.when(i < p_tiles)
    def _():
        tile(mp_ref, xp_ref, hp_ref)

    @pl.when(i >= p_tiles)
    def _():
        tile(ms_ref, xs_ref, hs_ref)


def _out_proj(mix_p, x_p, mix_s, x_s, w_out, norm_w, rw_hi, rw_lo, router_b):
    n_p, d = x_p.shape
    n_s = x_s.shape[0]
    tm = min(OUT_TM, n_p, n_s)
    p_tiles, s_tiles = n_p // tm, n_s // tm
    n_all = n_p + n_s
    pidx = lambda i: (jnp.minimum(i, p_tiles - 1), 0)
    sidx = lambda i: (jnp.maximum(i - p_tiles, 0), 0)
    const = lambda i: (0, 0)
    return pl.pallas_call(
        functools.partial(_out_proj_kernel, p_tiles=p_tiles),
        out_shape=(jax.ShapeDtypeStruct((n_p, d), F32),
                   jax.ShapeDtypeStruct((n_s, d), F32),
                   jax.ShapeDtypeStruct((n_all, d), F32),
                   jax.ShapeDtypeStruct((n_all, N_EXPERTS), F32)),
        grid=(p_tiles + s_tiles,),
        in_specs=[pl.BlockSpec((tm, MIX_W), pidx),
                  pl.BlockSpec((tm, d), pidx),
                  pl.BlockSpec((tm, MIX_W), sidx),
                  pl.BlockSpec((tm, d), sidx),
                  pl.BlockSpec((MIX_W, d), const),
                  pl.BlockSpec((1, d), const),
                  pl.BlockSpec((d, N_EXPERTS), const),
                  pl.BlockSpec((d, N_EXPERTS), const),
                  pl.BlockSpec((1, N_EXPERTS), const)],
        out_specs=(pl.BlockSpec((tm, d), pidx),
                   pl.BlockSpec((tm, d), sidx),
                   pl.BlockSpec((tm, d), lambda i: (i, 0)),
                   pl.BlockSpec((tm, N_EXPERTS), lambda i: (i, 0))),
        compiler_params=_cparams(("arbitrary",)),
        name="out_proj",
    )(mix_p, x_p, mix_s, x_s, w_out, norm_w, rw_hi, rw_lo, router_b)


def _route(logits, bm):
    n = logits.shape[0]
    tk = n * TOP_K
    top_val, top_idx = lax.top_k(logits, TOP_K)
    weights = jax.nn.softmax(top_val, axis=-1)
    flat_e = top_idx.reshape(tk).astype(jnp.int32)
    onehot = (flat_e[:, None] == jnp.arange(N_EXPERTS, dtype=jnp.int32)[None, :]).astype(jnp.int32)
    csum = jnp.cumsum(onehot, axis=0)
    rank = jnp.sum((csum - onehot) * onehot, axis=1)
    counts = csum[-1]
    padded = (counts + bm - 1) // bm * bm
    pad_end = jnp.cumsum(padded)
    pad_start = pad_end - padded
    dest = jnp.sum(onehot * pad_start[None, :], axis=1) + rank
    n_blocks = -(-(tk + N_EXPERTS * (bm - 1)) // bm)
    n_rows = n_blocks * bm
    block_e = jnp.minimum(jnp.searchsorted(pad_end, jnp.arange(n_blocks) * bm, side='right'),
                          N_EXPERTS - 1).astype(jnp.int32)
    row_tok = jnp.zeros((n_rows,), jnp.int32).at[dest].set(jnp.arange(tk, dtype=jnp.int32) // TOP_K)
    n_real = (pad_end[-1] // bm).astype(jnp.int32).reshape(1)
    return weights, dest.astype(jnp.int32), row_tok, block_e, n_real


def _row_copy(src_ref, dst_ref, sem, src_row, dst_row, n_rows=1):
    return pltpu.make_async_copy(src_ref.at[pl.ds(src_row, n_rows)],
                                 dst_ref.at[pl.ds(dst_row, n_rows)], sem)


def _moe_kernel(be_ref, nreal_ref, idx_ref, idx_next_ref, x_hbm, gw_ref, gb_ref, uw_ref, ub_ref,
                dw_ref, db_ref, y_ref, acc_ref, xb_ref, xg_ref, sem, *, bm):
    blk = pl.program_id(0)
    f = pl.program_id(1)
    last = pl.num_programs(1) - 1
    n_real = nreal_ref[0]
    slot = blk % 2

    def gather_start(rows_ref, s):
        def issue(i, carry):
            _row_copy(x_hbm, xg_ref.at[s], sem.at[s], rows_ref[0, 0, i], i).start()
            return carry
        lax.fori_loop(0, bm, issue, 0)

    @pl.when(blk < n_real)
    def _():
        @pl.when(f == 0)
        def _():
            @pl.when(blk == 0)
            def _():
                gather_start(idx_ref, 0)

            _row_copy(x_hbm, xg_ref.at[slot], sem.at[slot], 0, 0, bm).wait()
            xb_ref[...] = xg_ref[slot].astype(BF16)

            @pl.when(blk + 1 < n_real)
            def _():
                gather_start(idx_next_ref, 1 - slot)

        xb = xb_ref[...]
        g = jnp.minimum(_dot(xb, gw_ref[0]) + gb_ref[0], SWIGLU_LIMIT)
        u = jnp.clip(_dot(xb, uw_ref[0]) + ub_ref[0], -SWIGLU_LIMIT, SWIGLU_LIMIT)
        hmid = g * jax.nn.sigmoid(SWIGLU_ALPHA * g) * (u + 1.0)
        part = _dot(hmid.astype(BF16), dw_ref[0])

        @pl.when(f == 0)
        def _():
            acc_ref[...] = part

        @pl.when(f > 0)
        def _():
            acc_ref[...] += part

        @pl.when(f == last)
        def _():
            y_ref[...] = acc_ref[...] + db_ref[0]

    @pl.when(jnp.logical_and(blk >= n_real, f == last))
    def _():
        y_ref[...] = jnp.zeros_like(y_ref)


def _moe_experts(x_all, row_tok, block_e, n_real, gate_w, gate_b, up_w, up_b, down_w, down_b, *, bm):
    n_rows = row_tok.shape[0]
    d = x_all.shape[1]
    n_blocks = n_rows // bm
    d_ff = gate_w.shape[-1]
    tf = min(MOE_TF, d_ff)
    nf = d_ff // tf
    idx = row_tok.reshape(n_blocks, 1, bm)

    def e_of(b, be, nr):
        return be[jnp.minimum(b, nr[0] - 1)]

    def f_of(b, f, nr):
        return jnp.where(b < nr[0], f, nf - 1)

    return pl.pallas_call(
        functools.partial(_moe_kernel, bm=bm),
        out_shape=jax.ShapeDtypeStruct((n_rows, d), F32),
        grid_spec=pltpu.PrefetchScalarGridSpec(
            num_scalar_prefetch=2,
            grid=(n_blocks, nf),
            in_specs=[
                pl.BlockSpec((1, 1, bm), lambda b, f, be, nr: (b, 0, 0), memory_space=pltpu.SMEM),
                pl.BlockSpec((1, 1, bm), lambda b, f, be, nr: (jnp.minimum(b + 1, n_blocks - 1), 0, 0),
                             memory_space=pltpu.SMEM),
                pl.BlockSpec(memory_space=pl.ANY),
                pl.BlockSpec((1, d, tf), lambda b, f, be, nr: (e_of(b, be, nr), 0, f_of(b, f, nr))),
                pl.BlockSpec((1, 1, tf), lambda b, f, be, nr: (e_of(b, be, nr), 0, f_of(b, f, nr))),
                pl.BlockSpec((1, d, tf), lambda b, f, be, nr: (e_of(b, be, nr), 0, f_of(b, f, nr))),
                pl.BlockSpec((1, 1, tf), lambda b, f, be, nr: (e_of(b, be, nr), 0, f_of(b, f, nr))),
                pl.BlockSpec((1, tf, d), lambda b, f, be, nr: (e_of(b, be, nr), f_of(b, f, nr), 0)),
                pl.BlockSpec((1, 1, d), lambda b, f, be, nr: (e_of(b, be, nr), 0, 0)),
            ],
            out_specs=pl.BlockSpec((bm, d), lambda b, f, be, nr: (b, 0)),
            scratch_shapes=[pltpu.VMEM((bm, d), F32), pltpu.VMEM((bm, d), BF16),
                            pltpu.VMEM((2, bm, d), F32), pltpu.SemaphoreType.DMA((2,))]),
        compiler_params=_cparams(("arbitrary", "arbitrary")),
        name="moe_experts",
    )(block_e, n_real, idx, idx, x_all, gate_w, gate_b, up_w, up_b, down_w, down_b)


def _combine_kernel(idx_ref, w_ref, h_ref, nw_ref, rows_ref, out_ref, buf, sem, *, tm):
    def issue(i, carry):
        for k in range(TOP_K):
            _row_copy(rows_ref, buf.at[k], sem, idx_ref[0, 0, i * TOP_K + k], i).start()
        return carry

    lax.fori_loop(0, tm, issue, 0)
    for k in range(TOP_K):
        _row_copy(rows_ref, buf.at[k], sem, 0, 0, tm).wait()

    w = w_ref[...]
    y = w[:, 0:1] * buf[0]
    for k in range(1, TOP_K):
        y = y + w[:, k:k + 1] * buf[k]
    out_ref[...] = _rms(h_ref[...] + y) * nw_ref[...]


def _combine(y_rows, dest, weights, h, final_norm_w, *, tok_off):
    n, d = h.shape
    tm = min(COMBINE_TM, n)
    nb = dest.shape[0] // (tm * TOP_K)
    ob = tok_off // tm
    return pl.pallas_call(
        functools.partial(_combine_kernel, tm=tm),
        out_shape=jax.ShapeDtypeStruct((n, d), F32),
        grid=(n // tm,),
        in_specs=[pl.BlockSpec((1, 1, tm * TOP_K), lambda i: (i + ob, 0, 0), memory_space=pltpu.SMEM),
                  pl.BlockSpec((tm, TOP_K), lambda i: (i + ob, 0)),
                  pl.BlockSpec((tm, d), lambda i: (i, 0)),
                  pl.BlockSpec((1, d), lambda i: (0, 0)),
                  pl.BlockSpec(memory_space=pl.ANY)],
        out_specs=pl.BlockSpec((tm, d), lambda i: (i, 0)),
        scratch_shapes=[pltpu.VMEM((TOP_K, tm, d), F32), pltpu.SemaphoreType.DMA],
        compiler_params=_cparams(("arbitrary",)),
        name="moe_combine",
    )(dest.reshape(nb, 1, tm * TOP_K), weights, h, final_norm_w, y_rows)


def _split_w_in(w_in):
    lr0 = 2 * QA_W + VA_W
    w_main = jnp.concatenate([w_in[:, :lr0], w_in[:, lr0 + GLA_GATE_RANK:]], axis=1).astype(BF16)
    w_lr = jnp.pad(w_in[:, lr0:lr0 + GLA_GATE_RANK], ((0, 0), (0, LR_PAD - GLA_GATE_RANK))).astype(BF16)
    return w_main, w_lr


def kernel(x_prompt, x_sample, state_gla, cache_swa_k, cache_swa_v, attn_norm_w, w_in, gla_gate_up_w, gla_gate_b, gla_out_norm_w, swa_sinks, w_out, ffn_norm_w, router_w, router_b, expert_gate_w, expert_gate_b, expert_up_w, expert_up_b, expert_down_w, expert_down_b, final_norm_w):
    depth = w_in.shape[0]
    assert depth == 1, "the combine applies the final norm, so exactly one layer is supported"
    bp, tp, d = x_prompt.shape
    bs, ts, _ = x_sample.shape
    n_p, n_s = bp * tp, bs * ts
    hp = x_prompt.reshape(n_p, d)
    hs = x_sample.reshape(n_s, d)
    l = 0
    w_main, w_lr = _split_w_in(w_in[l])
    gup = jnp.pad(gla_gate_up_w[l], ((0, LR_PAD - GLA_GATE_RANK), (0, 0))).astype(BF16)
    gate_b = gla_gate_b[l].reshape(1, QA_W)
    onw = gla_out_norm_w[l].reshape(1, GLA_DV)
    anw = attn_norm_w[l].reshape(1, d)
    rw_hi = router_w[l].astype(BF16)
    rw_lo = (router_w[l] - rw_hi.astype(F32)).astype(BF16)

    proj_p, lr_p, kv_p = _in_proj(hp, anw, w_main, w_lr)
    mix_p, st_p = _mix_prompt(proj_p, lr_p, gup, gate_b, onw, swa_sinks[l], batch=bp, seq=tp)
    proj_s, lr_s, kv_s = _in_proj(hs, anw, w_main, w_lr)
    mix_s, st_s = _mix_sample(proj_s, lr_s, gup, gate_b, onw, swa_sinks[l],
                              jnp.swapaxes(state_gla[l], -1, -2),
                              cache_swa_k[l].reshape(bs, WINDOW, KB_W),
                              cache_swa_v[l].reshape(bs, WINDOW, KB_W), batch=bs, seq=ts)

    hp, hs, xn_all, lg_all = _out_proj(mix_p, hp, mix_s, hs, w_out[l].astype(BF16),
                                       ffn_norm_w[l].reshape(1, d), rw_hi, rw_lo,
                                       router_b[l].reshape(1, N_EXPERTS))

    weights, dest, row_tok, block_e, n_real = _route(lg_all, MOE_BM)
    y_rows = _moe_experts(xn_all, row_tok, block_e, n_real,
                          expert_gate_w[l].astype(BF16), expert_gate_b[l][:, None, :],
                          expert_up_w[l].astype(BF16), expert_up_b[l][:, None, :],
                          expert_down_w[l].astype(BF16), expert_down_b[l][:, None, :], bm=MOE_BM)
    fin = final_norm_w.reshape(1, d)
    y_prompt = _combine(y_rows, dest, weights, hp, fin, tok_off=0).reshape(bp, tp, d)
    y_sample = _combine(y_rows, dest, weights, hs, fin, tok_off=n_p).reshape(bs, ts, d)

    kvp = kv_p.reshape(bp, tp, 2, SWA_KV_HEADS, HEAD_DIM)[:, -WINDOW:]
    kvs = kv_s.reshape(bs, ts, 2, SWA_KV_HEADS, HEAD_DIM)
    return (y_prompt, y_sample,
            jnp.swapaxes(st_p, -1, -2)[None], kvp[:, :, 0][None], kvp[:, :, 1][None],
            jnp.swapaxes(st_s, -1, -2)[None],
            jnp.concatenate([cache_swa_k[l], kvs[:, :, 0]], axis=1)[:, -WINDOW:][None],
            jnp.concatenate([cache_swa_v[l], kvs[:, :, 1]], axis=1)[:, -WINDOW:][None])
```

```python
import functools

import jax
import jax.numpy as jnp
from jax import lax
from jax.experimental import pallas as pl
from jax.experimental.pallas import tpu as pltpu

F32 = jnp.float32
BF16 = jnp.bfloat16

CHUNK = 64
GLA_HEADS = 4
GLA_DK = 128
GLA_DV = 256
GLA_GATE_RANK = 16
GLA_GATE_NORM = 16.0
SWA_Q_HEADS = 16
SWA_KV_HEADS = 4
SWA_GROUP = SWA_Q_HEADS // SWA_KV_HEADS
HEAD_DIM = 64
WINDOW = 128
WINDOW_CHUNKS = WINDOW // CHUNK
ALIBI_MAX_BIAS = 8.0
N_EXPERTS = 32
TOP_K = 4
SWIGLU_LIMIT = 7.0
SWIGLU_ALPHA = 1.702
RMS_EPS = 1e-5
NEG_INF = -1e30

QA_W = GLA_HEADS * GLA_DK
VA_W = GLA_HEADS * GLA_DV
QB_W = SWA_Q_HEADS * HEAD_DIM
KB_W = SWA_KV_HEADS * HEAD_DIM
MIX_W = VA_W + QB_W
MAIN_W = 2 * QA_W + 2 * VA_W + QB_W + 2 * KB_W
QB_COL = (2 * QA_W + 2 * VA_W) // QB_W
KB_COL = (2 * QA_W + 2 * VA_W + QB_W) // KB_W
LANE = 128
LR_PAD = LANE

PROJ_TN = 1536
ROW_TILE = 512
MOE_BM = 512
MOE_TF = 1024
OUT_TM = 256
COMBINE_TM = 128
VMEM_LIMIT = 48 * 1024 * 1024
MOE_VMEM_LIMIT = 58 * 1024 * 1024


def _cparams(sem, vmem=VMEM_LIMIT):
    return pltpu.CompilerParams(dimension_semantics=sem, vmem_limit_bytes=vmem)


def _dot(a, b):
    return jnp.dot(a, b, preferred_element_type=F32)


def _dot_tn(a, b):
    return lax.dot_general(a, b, (((0,), (0,)), ((), ())), preferred_element_type=F32)


def _dot_nt(a, b):
    return lax.dot_general(a, b, (((1,), (1,)), ((), ())), preferred_element_type=F32)


def _rms(x):
    return x * lax.rsqrt(jnp.mean(x * x, axis=-1, keepdims=True) + RMS_EPS)


def _hi_lo(x):
    hi = x.astype(BF16)
    return hi, (x - hi.astype(F32)).astype(BF16)


def _in_proj_kernel(x_ref, nw_ref, w_ref, wlr_ref, out_ref, lr_ref, kv_ref, xn_ref, *, n_col_steps):
    j = pl.program_id(1)

    @pl.when(j == 0)
    def _():
        xnb = (_rms(x_ref[...]) * nw_ref[...]).astype(BF16)
        xn_ref[...] = xnb
        lr_ref[...] = _dot(xnb, wlr_ref[...]).astype(BF16)

    acc = _dot(xn_ref[...], w_ref[...])
    out_ref[...] = acc.astype(BF16)

    @pl.when(j == n_col_steps - 1)
    def _():
        kv_ref[...] = acc[:, PROJ_TN - 2 * KB_W:]


def _in_proj(x2d, norm_w, w_main, w_lr):
    n, d = x2d.shape
    tm = min(ROW_TILE, n)
    n_col_steps = MAIN_W // PROJ_TN
    return pl.pallas_call(
        functools.partial(_in_proj_kernel, n_col_steps=n_col_steps),
        out_shape=(jax.ShapeDtypeStruct((n, MAIN_W), BF16),
                   jax.ShapeDtypeStruct((n, LR_PAD), BF16),
                   jax.ShapeDtypeStruct((n, 2 * KB_W), F32)),
        grid=(n // tm, n_col_steps),
        in_specs=[pl.BlockSpec((tm, d), lambda i, j: (i, 0)),
                  pl.BlockSpec((1, d), lambda i, j: (0, 0)),
                  pl.BlockSpec((d, PROJ_TN), lambda i, j: (0, j)),
                  pl.BlockSpec((d, LR_PAD), lambda i, j: (0, 0))],
        out_specs=(pl.BlockSpec((tm, PROJ_TN), lambda i, j: (i, j)),
                   pl.BlockSpec((tm, LR_PAD), lambda i, j: (i, 0)),
                   pl.BlockSpec((tm, 2 * KB_W), lambda i, j: (i, 0))),
        scratch_shapes=[pltpu.VMEM((tm, d), BF16)],
        compiler_params=_cparams(("parallel", "arbitrary")),
        name="in_proj",
    )(x2d, norm_w, w_main, w_lr)


def _gla_chunk(rows, chunk, q_ref, k_ref, v_ref, ga_ref, lr_ref, gup, gb, nw, tri, s_scr, o_ref):
    z = _dot(lr_ref[rows, :], gup) + gb
    fc = (jnp.minimum(z, 0.0) - jnp.log1p(jnp.exp(-jnp.abs(z)))) * (1.0 / GLA_GATE_NORM)
    hi, lo = _hi_lo(fc)
    g = _dot(tri, hi) + _dot(tri, lo)
    g_end = g[chunk - 1:chunk, :]
    k_dec = (k_ref[rows, :].astype(F32) * jnp.exp(g_end - g)).astype(BF16)
    a_end = jnp.exp(g_end)
    for h in range(GLA_HEADS):
        ks = slice(h * GLA_DK, (h + 1) * GLA_DK)
        vs = slice(h * GLA_DV, (h + 1) * GLA_DV)
        upd = _dot_tn(v_ref[rows, vs], k_dec[:, ks])
        s_new = s_scr[h] * a_end[:, ks] + upd
        s_scr[h] = s_new
        o = _dot_nt(q_ref[rows, ks], s_new.astype(BF16)) * (GLA_DK ** -0.5)
        gate = ga_ref[rows, vs].astype(F32)
        o_ref[rows, vs] = (_rms(o) * nw * (gate * jax.nn.sigmoid(gate))).astype(BF16)


def _alibi_slopes():
    s = jnp.exp2(-ALIBI_MAX_BIAS * jnp.arange(1, SWA_Q_HEADS + 1, dtype=F32) / SWA_Q_HEADS)
    return s.reshape(SWA_KV_HEADS, SWA_GROUP)


def _stacked_bias(q_off, lq, lk):
    dist = jnp.abs(q_off + jnp.arange(lq)[:, None] - jnp.arange(lk)[None, :]).astype(F32)
    b = -_alibi_slopes()[:, :, None, None] * dist
    return b.reshape(SWA_KV_HEADS, SWA_GROUP * lq, lk)


def _stacked_sinks(sinks, lq):
    s = sinks.astype(F32).reshape(SWA_KV_HEADS, SWA_GROUP, 1)
    return jnp.broadcast_to(s, (SWA_KV_HEADS, SWA_GROUP, lq)).reshape(SWA_KV_HEADS, SWA_GROUP * lq, 1)


def _attend_group(q, keys, vals, bias, sink, valid):
    lq = q.shape[0]
    qs = jnp.concatenate([q[:, j * HEAD_DIM:(j + 1) * HEAD_DIM] for j in range(SWA_GROUP)], axis=0)
    s = _dot_nt(qs, keys) * (HEAD_DIM ** -0.5) + bias
    if valid is not None:
        s = jnp.where(valid, s, NEG_INF)
    m = jnp.maximum(jnp.max(s, axis=-1, keepdims=True), sink)
    p = jnp.exp(s - m)
    denom = jnp.sum(p, axis=-1, keepdims=True) + jnp.exp(sink - m)
    o = _dot(p.astype(BF16), vals) / denom
    return jnp.concatenate([o[j * lq:(j + 1) * lq, :] for j in range(SWA_GROUP)], axis=1)


def _mix_prompt_kernel(q_ref, k_ref, v_ref, ga_ref, lr_ref, qb_ref, kc_ref, vc_ref, kp_ref, vp_ref,
                       gup_ref, gb_ref, nw_ref, tri_ref, bias_ref, sink_ref,
                       o_ref, st_ref, s_scr, k_scr, v_scr, *, n_chunks):
    t = pl.program_id(1)
    back = WINDOW_CHUNKS * CHUNK
    span = back + CHUNK

    @pl.when(t == 0)
    def _():
        s_scr[...] = jnp.zeros_like(s_scr)

    k_scr[0:back, :] = kp_ref[...]
    v_scr[0:back, :] = vp_ref[...]
    k_scr[back:, :] = kc_ref[...]
    v_scr[back:, :] = vc_ref[...]
    key_idx = lax.broadcasted_iota(jnp.int32, (SWA_GROUP * CHUNK, span), 1)
    gup, gb, nw, tri = gup_ref[...], gb_ref[...], nw_ref[...], tri_ref[...]

    def chunk_step(c, carry):
        r0 = pl.multiple_of(c * CHUNK, CHUNK)
        rows = pl.ds(r0, CHUNK)
        _gla_chunk(rows, CHUNK, q_ref, k_ref, v_ref, ga_ref, lr_ref, gup, gb, nw, tri, s_scr, o_ref)
        valid = jnp.logical_or(t > 0, key_idx >= back - c * CHUNK)
        for g in range(SWA_KV_HEADS):
            hs = slice(g * HEAD_DIM, (g + 1) * HEAD_DIM)
            qs = slice(g * SWA_GROUP * HEAD_DIM, (g + 1) * SWA_GROUP * HEAD_DIM)
            o = _attend_group(qb_ref[rows, qs], k_scr[pl.ds(r0, span), hs],
                              v_scr[pl.ds(r0, span), hs], bias_ref[g], sink_ref[g], valid)
            o_ref[rows, VA_W + g * SWA_GROUP * HEAD_DIM:VA_W + (g + 1) * SWA_GROUP * HEAD_DIM] = o.astype(BF16)
        return carry

    lax.fori_loop(0, n_chunks, chunk_step, 0, unroll=2)

    @pl.when(t == pl.num_programs(1) - 1)
    def _():
        st_ref[0] = s_scr[...]


def _gla_consts(gup, gate_b, norm_w, chunk):
    tri = (jnp.arange(chunk)[:, None] >= jnp.arange(chunk)[None, :]).astype(BF16)
    specs = [pl.BlockSpec((LR_PAD, QA_W), lambda *_: (0, 0)),
             pl.BlockSpec((1, QA_W), lambda *_: (0, 0)),
             pl.BlockSpec((1, GLA_DV), lambda *_: (0, 0)),
             pl.BlockSpec((chunk, chunk), lambda *_: (0, 0))]
    return specs, [gup, gate_b, norm_w, tri]


def _mix_prompt(proj, lr, gup, gate_b, norm_w, sinks, *, batch, seq):
    rows = min(ROW_TILE, seq)
    n_chunks = rows // CHUNK
    nt = seq // rows
    back = WINDOW_CHUNKS * CHUNK
    pb = rows // back
    bias = _stacked_bias(back, CHUNK, back + CHUNK)
    sink = _stacked_sinks(sinks, CHUNK)
    rb = lambda b, t: b * nt + t
    prev = lambda b, t: jnp.maximum(rb(b, t) * pb - 1, 0)
    cspecs, cargs = _gla_consts(gup, gate_b, norm_w, CHUNK)
    return pl.pallas_call(
        functools.partial(_mix_prompt_kernel, n_chunks=n_chunks),
        out_shape=(jax.ShapeDtypeStruct((batch * seq, MIX_W), BF16),
                   jax.ShapeDtypeStruct((batch, GLA_HEADS, GLA_DV, GLA_DK), F32)),
        grid=(batch, nt),
        in_specs=[pl.BlockSpec((rows, QA_W), lambda b, t: (rb(b, t), 0)),
                  pl.BlockSpec((rows, QA_W), lambda b, t: (rb(b, t), 1)),
                  pl.BlockSpec((rows, VA_W), lambda b, t: (rb(b, t), 1)),
                  pl.BlockSpec((rows, VA_W), lambda b, t: (rb(b, t), 2)),
                  pl.BlockSpec((rows, LR_PAD), lambda b, t: (rb(b, t), 0)),
                  pl.BlockSpec((rows, QB_W), lambda b, t: (rb(b, t), QB_COL)),
                  pl.BlockSpec((rows, KB_W), lambda b, t: (rb(b, t), KB_COL)),
                  pl.BlockSpec((rows, KB_W), lambda b, t: (rb(b, t), KB_COL + 1)),
                  pl.BlockSpec((back, KB_W), lambda b, t: (prev(b, t), KB_COL)),
                  pl.BlockSpec((back, KB_W), lambda b, t: (prev(b, t), KB_COL + 1)),
                  *cspecs,
                  pl.BlockSpec(bias.shape, lambda b, t: (0, 0, 0)),
                  pl.BlockSpec(sink.shape, lambda b, t: (0, 0, 0))],
        out_specs=(pl.BlockSpec((rows, MIX_W), lambda b, t: (rb(b, t), 0)),
                   pl.BlockSpec((1, GLA_HEADS, GLA_DV, GLA_DK), lambda b, t: (b, 0, 0, 0))),
        scratch_shapes=[pltpu.VMEM((GLA_HEADS, GLA_DV, GLA_DK), F32),
                        pltpu.VMEM((rows + back, KB_W), BF16),
                        pltpu.VMEM((rows + back, KB_W), BF16)],
        compiler_params=_cparams(("parallel", "arbitrary")),
        name="mix_prompt",
    )(proj, proj, proj, proj, lr, proj, proj, proj, proj, proj, *cargs, bias, sink)


def _mix_sample_kernel(q_ref, k_ref, v_ref, ga_ref, lr_ref, qb_ref, kn_ref, vn_ref, wk_ref, wv_ref,
                       s0_ref, gup_ref, gb_ref, nw_ref, tri_ref, bias_ref, sink_ref,
                       o_ref, st_ref, s_scr, *, seq):
    s_scr[...] = s0_ref[0]
    _gla_chunk(slice(0, seq), seq, q_ref, k_ref, v_ref, ga_ref, lr_ref,
               gup_ref[...], gb_ref[...], nw_ref[...], tri_ref[...], s_scr, o_ref)
    st_ref[0] = s_scr[...]
    keys = jnp.concatenate([wk_ref[0].astype(BF16), kn_ref[...]], axis=0)
    vals = jnp.concatenate([wv_ref[0].astype(BF16), vn_ref[...]], axis=0)
    for g in range(SWA_KV_HEADS):
        hs = slice(g * HEAD_DIM, (g + 1) * HEAD_DIM)
        qs = slice(g * SWA_GROUP * HEAD_DIM, (g + 1) * SWA_GROUP * HEAD_DIM)
        o = _attend_group(qb_ref[:, qs], keys[:, hs], vals[:, hs], bias_ref[g], sink_ref[g], None)
        o_ref[:, VA_W + g * SWA_GROUP * HEAD_DIM:VA_W + (g + 1) * SWA_GROUP * HEAD_DIM] = o.astype(BF16)


def _mix_sample(proj, lr, gup, gate_b, norm_w, sinks, s0_t, win_k, win_v, *, batch, seq):
    bias = _stacked_bias(WINDOW, seq, WINDOW + seq)
    sink = _stacked_sinks(sinks, seq)
    cspecs, cargs = _gla_consts(gup, gate_b, norm_w, seq)
    state_spec = pl.BlockSpec((1, GLA_HEADS, GLA_DV, GLA_DK), lambda b: (b, 0, 0, 0))
    return pl.pallas_call(
        functools.partial(_mix_sample_kernel, seq=seq),
        out_shape=(jax.ShapeDtypeStruct((batch * seq, MIX_W), BF16),
                   jax.ShapeDtypeStruct((batch, GLA_HEADS, GLA_DV, GLA_DK), F32)),
        grid=(batch,),
        in_specs=[pl.BlockSpec((seq, QA_W), lambda b: (b, 0)),
                  pl.BlockSpec((seq, QA_W), lambda b: (b, 1)),
                  pl.BlockSpec((seq, VA_W), lambda b: (b, 1)),
                  pl.BlockSpec((seq, VA_W), lambda b: (b, 2)),
                  pl.BlockSpec((seq, LR_PAD), lambda b: (b, 0)),
                  pl.BlockSpec((seq, QB_W), lambda b: (b, QB_COL)),
                  pl.BlockSpec((seq, KB_W), lambda b: (b, KB_COL)),
                  pl.BlockSpec((seq, KB_W), lambda b: (b, KB_COL + 1)),
                  pl.BlockSpec((1, WINDOW, KB_W), lambda b: (b, 0, 0)),
                  pl.BlockSpec((1, WINDOW, KB_W), lambda b: (b, 0, 0)),
                  state_spec,
                  *cspecs,
                  pl.BlockSpec(bias.shape, lambda b: (0, 0, 0)),
                  pl.BlockSpec(sink.shape, lambda b: (0, 0, 0))],
        out_specs=(pl.BlockSpec((seq, MIX_W), lambda b: (b, 0)), state_spec),
        scratch_shapes=[pltpu.VMEM((GLA_HEADS, GLA_DV, GLA_DK), F32)],
        compiler_params=_cparams(("parallel",)),
        name="mix_sample",
    )(proj, proj, proj, proj, lr, proj, proj, proj, win_k, win_v, s0_t, *cargs, bias, sink)


def _out_proj_kernel(mp_ref, xp_ref, ms_ref, xs_ref, w_ref, nw_ref, rwh_ref, rwl_ref, rb_ref,
                     hp_ref, hs_ref, xn_ref, lg_ref, *, p_tiles):
    i = pl.program_id(0)

    def tile(m_ref, x_ref, h_ref):
        h = x_ref[...] + _dot(m_ref[...], w_ref[...])
        h_ref[...] = h
        xn = _rms(h) * nw_ref[...]
        xn_ref[...] = xn
        xh, xl = _hi_lo(xn)
        rwh = rwh_ref[...]
        lg_ref[...] = _dot(xh, rwh) + _dot(xl, rwh) + _dot(xh, rwl_ref[...]) + rb_ref[...]

    @pl.when(i < p_tiles)
    def _():
        tile(mp_ref, xp_ref, hp_ref)

    @pl.when(i >= p_tiles)
    def _():
        tile(ms_ref, xs_ref, hs_ref)


def _out_proj(mix_p, x_p, mix_s, x_s, w_out, norm_w, rw_hi, rw_lo, router_b):
    n_p, d = x_p.shape
    n_s = x_s.shape[0]
    tm = min(OUT_TM, n_p, n_s)
    p_tiles, s_tiles = n_p // tm, n_s // tm
    n_all = n_p + n_s
    pidx = lambda i: (jnp.minimum(i, p_tiles - 1), 0)
    sidx = lambda i: (jnp.maximum(i - p_tiles, 0), 0)
    const = lambda i: (0, 0)
    return pl.pallas_call(
        functools.partial(_out_proj_kernel, p_tiles=p_tiles),
        out_shape=(jax.ShapeDtypeStruct((n_p, d), F32),
                   jax.ShapeDtypeStruct((n_s, d), F32),
                   jax.ShapeDtypeStruct((n_all, d), F32),
                   jax.ShapeDtypeStruct((n_all, N_EXPERTS), F32)),
        grid=(p_tiles + s_tiles,),
        in_specs=[pl.BlockSpec((tm, MIX_W), pidx),
                  pl.BlockSpec((tm, d), pidx),
                  pl.BlockSpec((tm, MIX_W), sidx),
                  pl.BlockSpec((tm, d), sidx),
                  pl.BlockSpec((MIX_W, d), const),
                  pl.BlockSpec((1, d), const),
                  pl.BlockSpec((d, N_EXPERTS), const),
                  pl.BlockSpec((d, N_EXPERTS), const),
                  pl.BlockSpec((1, N_EXPERTS), const)],
        out_specs=(pl.BlockSpec((tm, d), pidx),
                   pl.BlockSpec((tm, d), sidx),
                   pl.BlockSpec((tm, d), lambda i: (i, 0)),
                   pl.BlockSpec((tm, N_EXPERTS), lambda i: (i, 0))),
        compiler_params=_cparams(("arbitrary",)),
        name="out_proj",
    )(mix_p, x_p, mix_s, x_s, w_out, norm_w, rw_hi, rw_lo, router_b)


def _route(logits, bm):
    n = logits.shape[0]
    tk = n * TOP_K
    top_val, top_idx = lax.top_k(logits, TOP_K)
    weights = jax.nn.softmax(top_val, axis=-1)
    flat_e = top_idx.reshape(tk).astype(jnp.int32)
    onehot = (flat_e[:, None] == jnp.arange(N_EXPERTS, dtype=jnp.int32)[None, :]).astype(jnp.int32)
    csum = jnp.cumsum(onehot, axis=0)
    rank = jnp.sum((csum - onehot) * onehot, axis=1)
    counts = csum[-1]
    padded = (counts + bm - 1) // bm * bm
    pad_end = jnp.cumsum(padded)
    pad_start = pad_end - padded
    dest = jnp.sum(onehot * pad_start[None, :], axis=1) + rank
    n_blocks = -(-(tk + N_EXPERTS * (bm - 1)) // bm)
    n_rows = n_blocks * bm
    block_e = jnp.minimum(jnp.searchsorted(pad_end, jnp.arange(n_blocks) * bm, side='right'),
                          N_EXPERTS - 1).astype(jnp.int32)
    asg = jnp.arange(tk, dtype=jnp.int32)
    row_asg = jnp.full((n_rows,), -1, jnp.int32).at[dest].set(asg)
    is_pad = row_asg < 0
    row_tok = jnp.where(is_pad, 0, row_asg // TOP_K)
    row_slot = jnp.where(is_pad, tk + jnp.arange(n_rows, dtype=jnp.int32) % bm,
                         (row_asg % TOP_K) * n + row_asg // TOP_K)
    n_real = (pad_end[-1] // bm).astype(jnp.int32).reshape(1)
    return weights, row_tok, row_slot, block_e, n_real


def _row_copy(src_ref, dst_ref, sem, src_row, dst_row, n_rows=1):
    return pltpu.make_async_copy(src_ref.at[pl.ds(src_row, n_rows)],
                                 dst_ref.at[pl.ds(dst_row, n_rows)], sem)


def _moe_kernel(be_ref, nreal_ref, idx_ref, idx_next_ref, slot_prev_ref, x_hbm,
                gw_ref, gb_ref, uw_ref, ub_ref, dw_ref, db_ref, ys_hbm,
                acc_ref, xb_ref, xg_ref, yb_ref, gsem, ssem, *, bm, nf):
    blk = pl.program_id(0)
    f = pl.program_id(1)
    q = bm // nf
    n_real = nreal_ref[0]
    cur = blk % 2
    oth = 1 - cur

    def gather(rows_ref, s, i):
        return _row_copy(x_hbm, xg_ref.at[s], gsem.at[s], rows_ref[0, 0, i], i)

    def scatter(s, i):
        return _row_copy(yb_ref.at[s], ys_hbm, ssem.at[s], i, slot_prev_ref[0, 0, i])

    def wait_gather(s):
        _row_copy(x_hbm, xg_ref.at[s], gsem.at[s], 0, 0, bm).wait()

    def wait_scatter(s):
        _row_copy(yb_ref.at[s], ys_hbm, ssem.at[s], 0, 0, bm).wait()

    def start_all(make):
        def body(i, carry):
            make(i).start()
            return carry
        lax.fori_loop(0, bm, body, 0)

    @pl.when(blk < n_real)
    def _():
        @pl.when(f == 0)
        def _():
            @pl.when(blk == 0)
            def _():
                yb_ref[1] = jnp.zeros(yb_ref.shape[1:], yb_ref.dtype)
                start_all(lambda i: gather(idx_ref, 0, i))

            wait_gather(cur)
            xb_ref[...] = xg_ref[cur].astype(BF16)

            @pl.when(blk >= 1)
            def _():
                wait_scatter(cur)

        base = f * q
        for i in range(q):
            gather(idx_next_ref, oth, base + i).start()
            scatter(oth, base + i).start()

        xb = xb_ref[...]
        g = jnp.minimum(_dot(xb, gw_ref[0]) + gb_ref[0], SWIGLU_LIMIT)
        u = jnp.clip(_dot(xb, uw_ref[0]) + ub_ref[0], -SWIGLU_LIMIT, SWIGLU_LIMIT)
        hmid = g * jax.nn.sigmoid(SWIGLU_ALPHA * g) * (u + 1.0)
        part = _dot(hmid.astype(BF16), dw_ref[0])

        @pl.when(f == 0)
        def _():
            acc_ref[...] = part

        @pl.when(f > 0)
        def _():
            acc_ref[...] += part

        @pl.when(f == nf - 1)
        def _():
            yb_ref[cur] = acc_ref[...] + db_ref[0]

    @pl.when(jnp.logical_and(blk == n_real, f == 0))
    def _():
        wait_gather(cur)
        start_all(lambda i: scatter(oth, i))
        wait_scatter(cur)
        wait_scatter(oth)


def _moe_experts(x_all, row_tok, row_slot, block_e, n_real, gate_w, gate_b, up_w, up_b, down_w, down_b,
                 *, bm):
    n_rows = row_tok.shape[0]
    n, d = x_all.shape
    n_blocks = n_rows // bm
    d_ff = gate_w.shape[-1]
    tf = min(MOE_TF, d_ff)
    nf = d_ff // tf
    idx = row_tok.reshape(n_blocks, 1, bm)
    trash = TOP_K * n + jnp.arange(bm, dtype=jnp.int32)
    slot_prev = jnp.concatenate([trash, row_slot]).reshape(n_blocks + 1, 1, bm)

    def e_of(b, be, nr):
        return be[jnp.minimum(b, nr[0] - 1)]

    def f_of(b, f, nr):
        return jnp.where(b < nr[0], f, nf - 1)

    smem = functools.partial(pl.BlockSpec, (1, 1, bm), memory_space=pltpu.SMEM)
    return pl.pallas_call(
        functools.partial(_moe_kernel, bm=bm, nf=nf),
        out_shape=jax.ShapeDtypeStruct((TOP_K * n + bm, d), F32),
        grid_spec=pltpu.PrefetchScalarGridSpec(
            num_scalar_prefetch=2,
            grid=(n_blocks, nf),
            in_specs=[
                smem(lambda b, f, be, nr: (b, 0, 0)),
                smem(lambda b, f, be, nr: (jnp.minimum(b + 1, n_blocks - 1), 0, 0)),
                smem(lambda b, f, be, nr: (b, 0, 0)),
                pl.BlockSpec(memory_space=pl.ANY),
                pl.BlockSpec((1, d, tf), lambda b, f, be, nr: (e_of(b, be, nr), 0, f_of(b, f, nr))),
                pl.BlockSpec((1, 1, tf), lambda b, f, be, nr: (e_of(b, be, nr), 0, f_of(b, f, nr))),
                pl.BlockSpec((1, d, tf), lambda b, f, be, nr: (e_of(b, be, nr), 0, f_of(b, f, nr))),
                pl.BlockSpec((1, 1, tf), lambda b, f, be, nr: (e_of(b, be, nr), 0, f_of(b, f, nr))),
                pl.BlockSpec((1, tf, d), lambda b, f, be, nr: (e_of(b, be, nr), f_of(b, f, nr), 0)),
                pl.BlockSpec((1, 1, d), lambda b, f, be, nr: (e_of(b, be, nr), 0, 0)),
            ],
            out_specs=pl.BlockSpec(memory_space=pl.ANY),
            scratch_shapes=[pltpu.VMEM((bm, d), F32), pltpu.VMEM((bm, d), BF16),
                            pltpu.VMEM((2, bm, d), F32), pltpu.VMEM((2, bm, d), F32),
                            pltpu.SemaphoreType.DMA((2,)), pltpu.SemaphoreType.DMA((2,))]),
        compiler_params=_cparams(("arbitrary", "arbitrary"), MOE_VMEM_LIMIT),
        name="moe_experts",
    )(block_e, n_real, idx, idx, slot_prev, x_all, gate_w, gate_b, up_w, up_b, down_w, down_b)


def _combine_kernel(*refs):
    y_refs, (w_ref, h_ref, nw_ref, out_ref) = refs[:TOP_K], refs[TOP_K:]
    w = w_ref[...]
    y = w[:, 0:1] * y_refs[0][...]
    for k in range(1, TOP_K):
        y = y + w[:, k:k + 1] * y_refs[k][...]
    out_ref[...] = _rms(h_ref[...] + y) * nw_ref[...]


def _combine(y_slots, weights, h, final_norm_w, *, tok_off):
    n, d = h.shape
    n_all = weights.shape[0]
    tm = min(COMBINE_TM, n)
    ob = tok_off // tm
    slot_blocks = n_all // tm
    y_specs = [pl.BlockSpec((tm, d), functools.partial(lambda k, i: (k * slot_blocks + ob + i, 0), k))
               for k in range(TOP_K)]
    return pl.pallas_call(
        _combine_kernel,
        out_shape=jax.ShapeDtypeStruct((n, d), F32),
        grid=(n // tm,),
        in_specs=[*y_specs,
                  pl.BlockSpec((tm, TOP_K), lambda i: (i + ob, 0)),
                  pl.BlockSpec((tm, d), lambda i: (i, 0)),
                  pl.BlockSpec((1, d), lambda i: (0, 0))],
        out_specs=pl.BlockSpec((tm, d), lambda i: (i, 0)),
        compiler_params=_cparams(("parallel",)),
        name="moe_combine",
    )(*([y_slots] * TOP_K), weights, h, final_norm_w)


def _split_w_in(w_in):
    lr0 = 2 * QA_W + VA_W
    w_main = jnp.concatenate([w_in[:, :lr0], w_in[:, lr0 + GLA_GATE_RANK:]], axis=1).astype(BF16)
    w_lr = jnp.pad(w_in[:, lr0:lr0 + GLA_GATE_RANK], ((0, 0), (0, LR_PAD - GLA_GATE_RANK))).astype(BF16)
    return w_main, w_lr


def kernel(x_prompt, x_sample, state_gla, cache_swa_k, cache_swa_v, attn_norm_w, w_in, gla_gate_up_w, gla_gate_b, gla_out_norm_w, swa_sinks, w_out, ffn_norm_w, router_w, router_b, expert_gate_w, expert_gate_b, expert_up_w, expert_up_b, expert_down_w, expert_down_b, final_norm_w):
    depth = w_in.shape[0]
    assert depth == 1, "the combine applies the final norm, so exactly one layer is supported"
    bp, tp, d = x_prompt.shape
    bs, ts, _ = x_sample.shape
    n_p, n_s = bp * tp, bs * ts
    hp = x_prompt.reshape(n_p, d)
    hs = x_sample.reshape(n_s, d)
    l = 0
    w_main, w_lr = _split_w_in(w_in[l])
    gup = jnp.pad(gla_gate_up_w[l], ((0, LR_PAD - GLA_GATE_RANK), (0, 0))).astype(BF16)
    gate_b = gla_gate_b[l].reshape(1, QA_W)
    onw = gla_out_norm_w[l].reshape(1, GLA_DV)
    anw = attn_norm_w[l].reshape(1, d)
    rw_hi = router_w[l].astype(BF16)
    rw_lo = (router_w[l] - rw_hi.astype(F32)).astype(BF16)

    proj_p, lr_p, kv_p = _in_proj(hp, anw, w_main, w_lr)
    mix_p, st_p = _mix_prompt(proj_p, lr_p, gup, gate_b, onw, swa_sinks[l], batch=bp, seq=tp)
    proj_s, lr_s, kv_s = _in_proj(hs, anw, w_main, w_lr)
    mix_s, st_s = _mix_sample(proj_s, lr_s, gup, gate_b, onw, swa_sinks[l],
                              jnp.swapaxes(state_gla[l], -1, -2),
                              cache_swa_k[l].reshape(bs, WINDOW, KB_W),
                              cache_swa_v[l].reshape(bs, WINDOW, KB_W), batch=bs, seq=ts)

    hp, hs, xn_all, lg_all = _out_proj(mix_p, hp, mix_s, hs, w_out[l].astype(BF16),
                                       ffn_norm_w[l].reshape(1, d), rw_hi, rw_lo,
                                       router_b[l].reshape(1, N_EXPERTS))

    weights, row_tok, row_slot, block_e, n_real = _route(lg_all, MOE_BM)
    y_slots = _moe_experts(xn_all, row_tok, row_slot, block_e, n_real,
                          expert_gate_w[l].astype(BF16), expert_gate_b[l][:, None, :],
                          expert_up_w[l].astype(BF16), expert_up_b[l][:, None, :],
                          expert_down_w[l].astype(BF16), expert_down_b[l][:, None, :], bm=MOE_BM)
    fin = final_norm_w.reshape(1, d)
    y_prompt = _combine(y_slots, weights, hp, fin, tok_off=0).reshape(bp, tp, d)
    y_sample = _combine(y_slots, weights, hs, fin, tok_off=n_p).reshape(bs, ts, d)

    kvp = kv_p.reshape(bp, tp, 2, SWA_KV_HEADS, HEAD_DIM)[:, -WINDOW:]
    kvs = kv_s.reshape(bs, ts, 2, SWA_KV_HEADS, HEAD_DIM)
    return (y_prompt, y_sample,
            jnp.swapaxes(st_p, -1, -2)[None], kvp[:, :, 0][None], kvp[:, :, 1][None],
            jnp.swapaxes(st_s, -1, -2)[None],
            jnp.concatenate([cache_swa_k[l], kvs[:, :, 0]], axis=1)[:, -WINDOW:][None],
            jnp.concatenate([cache_swa_v[l], kvs[:, :, 1]], axis=1)[:, -WINDOW:][None])
```

```python
import functools

import jax
import jax.numpy as jnp
from jax import lax
from jax.experimental import pallas as pl
from jax.experimental.pallas import tpu as pltpu

F32 = jnp.float32
BF16 = jnp.bfloat16

CHUNK = 64
GLA_HEADS = 4
GLA_DK = 128
GLA_DV = 256
GLA_GATE_RANK = 16
GLA_GATE_NORM = 16.0
SWA_Q_HEADS = 16
SWA_KV_HEADS = 4
SWA_GROUP = SWA_Q_HEADS // SWA_KV_HEADS
HEAD_DIM = 64
WINDOW = 128
WINDOW_CHUNKS = WINDOW // CHUNK
ALIBI_MAX_BIAS = 8.0
N_EXPERTS = 32
TOP_K = 4
SWIGLU_LIMIT = 7.0
SWIGLU_ALPHA = 1.702
RMS_EPS = 1e-5
NEG_INF = -1e30

QA_W = GLA_HEADS * GLA_DK
VA_W = GLA_HEADS * GLA_DV
QB_W = SWA_Q_HEADS * HEAD_DIM
KB_W = SWA_KV_HEADS * HEAD_DIM
MIX_W = VA_W + QB_W
MAIN_W = 2 * QA_W + 2 * VA_W + QB_W + 2 * KB_W
QB_COL = (2 * QA_W + 2 * VA_W) // QB_W
KB_COL = (2 * QA_W + 2 * VA_W + QB_W) // KB_W
LANE = 128
LR_PAD = LANE

PROJ_TN = 1536
ROW_TILE = 512
MOE_BM = 512
MOE_TF = 1024
OUT_TM = 256
COMBINE_TM = 128
VMEM_LIMIT = 48 * 1024 * 1024
MOE_VMEM_LIMIT = 58 * 1024 * 1024


def _cparams(sem, vmem=VMEM_LIMIT):
    return pltpu.CompilerParams(dimension_semantics=sem, vmem_limit_bytes=vmem)


def _dot(a, b):
    return jnp.dot(a, b, preferred_element_type=F32)


def _dot_tn(a, b):
    return lax.dot_general(a, b, (((0,), (0,)), ((), ())), preferred_element_type=F32)


def _dot_nt(a, b):
    return lax.dot_general(a, b, (((1,), (1,)), ((), ())), preferred_element_type=F32)


def _rms(x):
    return x * lax.rsqrt(jnp.mean(x * x, axis=-1, keepdims=True) + RMS_EPS)


def _hi_lo(x):
    hi = x.astype(BF16)
    return hi, (x - hi.astype(F32)).astype(BF16)


def _in_proj_kernel(x_ref, nw_ref, w_ref, wlr_ref, out_ref, lr_ref, kv_ref, xn_ref, *, n_col_steps):
    j = pl.program_id(1)

    @pl.when(j == 0)
    def _():
        xnb = (_rms(x_ref[...]) * nw_ref[...]).astype(BF16)
        xn_ref[...] = xnb
        lr_ref[...] = _dot(xnb, wlr_ref[...]).astype(BF16)

    acc = _dot(xn_ref[...], w_ref[...])
    out_ref[...] = acc.astype(BF16)

    @pl.when(j == n_col_steps - 1)
    def _():
        kv_ref[...] = acc[:, PROJ_TN - 2 * KB_W:]


def _in_proj(x2d, norm_w, w_main, w_lr):
    n, d = x2d.shape
    tm = min(ROW_TILE, n)
    n_col_steps = MAIN_W // PROJ_TN
    return pl.pallas_call(
        functools.partial(_in_proj_kernel, n_col_steps=n_col_steps),
        out_shape=(jax.ShapeDtypeStruct((n, MAIN_W), BF16),
                   jax.ShapeDtypeStruct((n, LR_PAD), BF16),
                   jax.ShapeDtypeStruct((n, 2 * KB_W), F32)),
        grid=(n // tm, n_col_steps),
        in_specs=[pl.BlockSpec((tm, d), lambda i, j: (i, 0)),
                  pl.BlockSpec((1, d), lambda i, j: (0, 0)),
                  pl.BlockSpec((d, PROJ_TN), lambda i, j: (0, j)),
                  pl.BlockSpec((d, LR_PAD), lambda i, j: (0, 0))],
        out_specs=(pl.BlockSpec((tm, PROJ_TN), lambda i, j: (i, j)),
                   pl.BlockSpec((tm, LR_PAD), lambda i, j: (i, 0)),
                   pl.BlockSpec((tm, 2 * KB_W), lambda i, j: (i, 0))),
        scratch_shapes=[pltpu.VMEM((tm, d), BF16)],
        compiler_params=_cparams(("parallel", "arbitrary")),
        name="in_proj",
    )(x2d, norm_w, w_main, w_lr)


def _gla_chunk(rows, chunk, q_ref, k_ref, v_ref, ga_ref, lr_ref, gup, gb, nw, tri, s_scr, o_ref):
    z = _dot(lr_ref[rows, :], gup) + gb
    fc = (jnp.minimum(z, 0.0) - jnp.log1p(jnp.exp(-jnp.abs(z)))) * (1.0 / GLA_GATE_NORM)
    hi, lo = _hi_lo(fc)
    g = _dot(tri, hi) + _dot(tri, lo)
    g_end = g[chunk - 1:chunk, :]
    k_dec = (k_ref[rows, :].astype(F32) * jnp.exp(g_end - g)).astype(BF16)
    a_end = jnp.exp(g_end)
    for h in range(GLA_HEADS):
        ks = slice(h * GLA_DK, (h + 1) * GLA_DK)
        vs = slice(h * GLA_DV, (h + 1) * GLA_DV)
        upd = _dot_tn(v_ref[rows, vs], k_dec[:, ks])
        s_new = s_scr[h] * a_end[:, ks] + upd
        s_scr[h] = s_new
        o = _dot_nt(q_ref[rows, ks], s_new.astype(BF16)) * (GLA_DK ** -0.5)
        gate = ga_ref[rows, vs].astype(F32)
        o_ref[rows, vs] = (_rms(o) * nw * (gate * jax.nn.sigmoid(gate))).astype(BF16)


def _alibi_slopes():
    s = jnp.exp2(-ALIBI_MAX_BIAS * jnp.arange(1, SWA_Q_HEADS + 1, dtype=F32) / SWA_Q_HEADS)
    return s.reshape(SWA_KV_HEADS, SWA_GROUP)


def _stacked_bias(q_off, lq, lk):
    dist = jnp.abs(q_off + jnp.arange(lq)[:, None] - jnp.arange(lk)[None, :]).astype(F32)
    b = -_alibi_slopes()[:, :, None, None] * dist
    return b.reshape(SWA_KV_HEADS, SWA_GROUP * lq, lk)


def _stacked_sinks(sinks, lq):
    s = sinks.astype(F32).reshape(SWA_KV_HEADS, SWA_GROUP, 1)
    return jnp.broadcast_to(s, (SWA_KV_HEADS, SWA_GROUP, lq)).reshape(SWA_KV_HEADS, SWA_GROUP * lq, 1)


def _attend_group(q, keys, vals, bias, sink, valid):
    lq = q.shape[0]
    qs = jnp.concatenate([q[:, j * HEAD_DIM:(j + 1) * HEAD_DIM] for j in range(SWA_GROUP)], axis=0)
    s = _dot_nt(qs, keys) * (HEAD_DIM ** -0.5) + bias
    if valid is not None:
        s = jnp.where(valid, s, NEG_INF)
    m = jnp.maximum(jnp.max(s, axis=-1, keepdims=True), sink)
    p = jnp.exp(s - m)
    denom = jnp.sum(p, axis=-1, keepdims=True) + jnp.exp(sink - m)
    o = _dot(p.astype(BF16), vals) / denom
    return jnp.concatenate([o[j * lq:(j + 1) * lq, :] for j in range(SWA_GROUP)], axis=1)


def _mix_prompt_kernel(q_ref, k_ref, v_ref, ga_ref, lr_ref, qb_ref, kc_ref, vc_ref, kp_ref, vp_ref,
                       gup_ref, gb_ref, nw_ref, tri_ref, bias_ref, sink_ref,
                       o_ref, st_ref, s_scr, k_scr, v_scr, *, n_chunks):
    t = pl.program_id(1)
    back = WINDOW_CHUNKS * CHUNK
    span = back + CHUNK

    @pl.when(t == 0)
    def _():
        s_scr[...] = jnp.zeros_like(s_scr)

    k_scr[0:back, :] = kp_ref[...]
    v_scr[0:back, :] = vp_ref[...]
    k_scr[back:, :] = kc_ref[...]
    v_scr[back:, :] = vc_ref[...]
    key_idx = lax.broadcasted_iota(jnp.int32, (SWA_GROUP * CHUNK, span), 1)
    gup, gb, nw, tri = gup_ref[...], gb_ref[...], nw_ref[...], tri_ref[...]

    def chunk_step(c, carry):
        r0 = pl.multiple_of(c * CHUNK, CHUNK)
        rows = pl.ds(r0, CHUNK)
        _gla_chunk(rows, CHUNK, q_ref, k_ref, v_ref, ga_ref, lr_ref, gup, gb, nw, tri, s_scr, o_ref)
        valid = jnp.logical_or(t > 0, key_idx >= back - c * CHUNK)
        for g in range(SWA_KV_HEADS):
            hs = slice(g * HEAD_DIM, (g + 1) * HEAD_DIM)
            qs = slice(g * SWA_GROUP * HEAD_DIM, (g + 1) * SWA_GROUP * HEAD_DIM)
            o = _attend_group(qb_ref[rows, qs], k_scr[pl.ds(r0, span), hs],
                              v_scr[pl.ds(r0, span), hs], bias_ref[g], sink_ref[g], valid)
            o_ref[rows, VA_W + g * SWA_GROUP * HEAD_DIM:VA_W + (g + 1) * SWA_GROUP * HEAD_DIM] = o.astype(BF16)
        return carry

    lax.fori_loop(0, n_chunks, chunk_step, 0, unroll=2)

    @pl.when(t == pl.num_programs(1) - 1)
    def _():
        st_ref[0] = s_scr[...]


def _gla_consts(gup, gate_b, norm_w, chunk):
    tri = (jnp.arange(chunk)[:, None] >= jnp.arange(chunk)[None, :]).astype(BF16)
    specs = [pl.BlockSpec((LR_PAD, QA_W), lambda *_: (0, 0)),
             pl.BlockSpec((1, QA_W), lambda *_: (0, 0)),
             pl.BlockSpec((1, GLA_DV), lambda *_: (0, 0)),
             pl.BlockSpec((chunk, chunk), lambda *_: (0, 0))]
    return specs, [gup, gate_b, norm_w, tri]


def _mix_prompt(proj, lr, gup, gate_b, norm_w, sinks, *, batch, seq):
    rows = min(ROW_TILE, seq)
    n_chunks = rows // CHUNK
    nt = seq // rows
    back = WINDOW_CHUNKS * CHUNK
    pb = rows // back
    bias = _stacked_bias(back, CHUNK, back + CHUNK)
    sink = _stacked_sinks(sinks, CHUNK)
    rb = lambda b, t: b * nt + t
    prev = lambda b, t: jnp.maximum(rb(b, t) * pb - 1, 0)
    cspecs, cargs = _gla_consts(gup, gate_b, norm_w, CHUNK)
    return pl.pallas_call(
        functools.partial(_mix_prompt_kernel, n_chunks=n_chunks),
        out_shape=(jax.ShapeDtypeStruct((batch * seq, MIX_W), BF16),
                   jax.ShapeDtypeStruct((batch, GLA_HEADS, GLA_DV, GLA_DK), F32)),
        grid=(batch, nt),
        in_specs=[pl.BlockSpec((rows, QA_W), lambda b, t: (rb(b, t), 0)),
                  pl.BlockSpec((rows, QA_W), lambda b, t: (rb(b, t), 1)),
                  pl.BlockSpec((rows, VA_W), lambda b, t: (rb(b, t), 1)),
                  pl.BlockSpec((rows, VA_W), lambda b, t: (rb(b, t), 2)),
                  pl.BlockSpec((rows, LR_PAD), lambda b, t: (rb(b, t), 0)),
                  pl.BlockSpec((rows, QB_W), lambda b, t: (rb(b, t), QB_COL)),
                  pl.BlockSpec((rows, KB_W), lambda b, t: (rb(b, t), KB_COL)),
                  pl.BlockSpec((rows, KB_W), lambda b, t: (rb(b, t), KB_COL + 1)),
                  pl.BlockSpec((back, KB_W), lambda b, t: (prev(b, t), KB_COL)),
                  pl.BlockSpec((back, KB_W), lambda b, t: (prev(b, t), KB_COL + 1)),
                  *cspecs,
                  pl.BlockSpec(bias.shape, lambda b, t: (0, 0, 0)),
                  pl.BlockSpec(sink.shape, lambda b, t: (0, 0, 0))],
        out_specs=(pl.BlockSpec((rows, MIX_W), lambda b, t: (rb(b, t), 0)),
                   pl.BlockSpec((1, GLA_HEADS, GLA_DV, GLA_DK), lambda b, t: (b, 0, 0, 0))),
        scratch_shapes=[pltpu.VMEM((GLA_HEADS, GLA_DV, GLA_DK), F32),
                        pltpu.VMEM((rows + back, KB_W), BF16),
                        pltpu.VMEM((rows + back, KB_W), BF16)],
        compiler_params=_cparams(("parallel", "arbitrary")),
        name="mix_prompt",
    )(proj, proj, proj, proj, lr, proj, proj, proj, proj, proj, *cargs, bias, sink)


def _mix_sample_kernel(q_ref, k_ref, v_ref, ga_ref, lr_ref, qb_ref, kn_ref, vn_ref, wk_ref, wv_ref,
                       s0_ref, gup_ref, gb_ref, nw_ref, tri_ref, bias_ref, sink_ref,
                       o_ref, st_ref, s_scr, *, seq):
    s_scr[...] = s0_ref[0]
    _gla_chunk(slice(0, seq), seq, q_ref, k_ref, v_ref, ga_ref, lr_ref,
               gup_ref[...], gb_ref[...], nw_ref[...], tri_ref[...], s_scr, o_ref)
    st_ref[0] = s_scr[...]
    keys = jnp.concatenate([wk_ref[0].astype(BF16), kn_ref[...]], axis=0)
    vals = jnp.concatenate([wv_ref[0].astype(BF16), vn_ref[...]], axis=0)
    for g in range(SWA_KV_HEADS):
        hs = slice(g * HEAD_DIM, (g + 1) * HEAD_DIM)
        qs = slice(g * SWA_GROUP * HEAD_DIM, (g + 1) * SWA_GROUP * HEAD_DIM)
        o = _attend_group(qb_ref[:, qs], keys[:, hs], vals[:, hs], bias_ref[g], sink_ref[g], None)
        o_ref[:, VA_W + g * SWA_GROUP * HEAD_DIM:VA_W + (g + 1) * SWA_GROUP * HEAD_DIM] = o.astype(BF16)


def _mix_sample(proj, lr, gup, gate_b, norm_w, sinks, s0_t, win_k, win_v, *, batch, seq):
    bias = _stacked_bias(WINDOW, seq, WINDOW + seq)
    sink = _stacked_sinks(sinks, seq)
    cspecs, cargs = _gla_consts(gup, gate_b, norm_w, seq)
    state_spec = pl.BlockSpec((1, GLA_HEADS, GLA_DV, GLA_DK), lambda b: (b, 0, 0, 0))
    return pl.pallas_call(
        functools.partial(_mix_sample_kernel, seq=seq),
        out_shape=(jax.ShapeDtypeStruct((batch * seq, MIX_W), BF16),
                   jax.ShapeDtypeStruct((batch, GLA_HEADS, GLA_DV, GLA_DK), F32)),
        grid=(batch,),
        in_specs=[pl.BlockSpec((seq, QA_W), lambda b: (b, 0)),
                  pl.BlockSpec((seq, QA_W), lambda b: (b, 1)),
                  pl.BlockSpec((seq, VA_W), lambda b: (b, 1)),
                  pl.BlockSpec((seq, VA_W), lambda b: (b, 2)),
                  pl.BlockSpec((seq, LR_PAD), lambda b: (b, 0)),
                  pl.BlockSpec((seq, QB_W), lambda b: (b, QB_COL)),
                  pl.BlockSpec((seq, KB_W), lambda b: (b, KB_COL)),
                  pl.BlockSpec((seq, KB_W), lambda b: (b, KB_COL + 1)),
                  pl.BlockSpec((1, WINDOW, KB_W), lambda b: (b, 0, 0)),
                  pl.BlockSpec((1, WINDOW, KB_W), lambda b: (b, 0, 0)),
                  state_spec,
                  *cspecs,
                  pl.BlockSpec(bias.shape, lambda b: (0, 0, 0)),
                  pl.BlockSpec(sink.shape, lambda b: (0, 0, 0))],
        out_specs=(pl.BlockSpec((seq, MIX_W), lambda b: (b, 0)), state_spec),
        scratch_shapes=[pltpu.VMEM((GLA_HEADS, GLA_DV, GLA_DK), F32)],
        compiler_params=_cparams(("parallel",)),
        name="mix_sample",
    )(proj, proj, proj, proj, lr, proj, proj, proj, win_k, win_v, s0_t, *cargs, bias, sink)


def _out_proj_kernel(mp_ref, xp_ref, ms_ref, xs_ref, w_ref, nw_ref, rwh_ref, rwl_ref, rb_ref,
                     hp_ref, hs_ref, xn_ref, lg_ref, *, p_tiles):
    i = pl.program_id(0)

    def tile(m_ref, x_ref, h_ref):
        h = x_ref[...] + _dot(m_ref[...], w_ref[...])
        h_ref[...] = h
        xn = _rms(h) * nw_ref[...]
        xn_ref[...] = xn
        xh, xl = _hi_lo(xn)
        rwh = rwh_ref[...]
        lg_ref[...] = _dot(xh, rwh) + _dot(xl, rwh) + _dot(xh, rwl_ref[...]) + rb_ref[...]

    @pl.when(i < p_tiles)
    def _():
        tile(mp_ref, xp_ref, hp_ref)

    @pl.when(i >= p_tiles)
    def _():
        tile(ms_ref, xs_ref, hs_ref)


def _out_proj(mix_p, x_p, mix_s, x_s, w_out, norm_w, rw_hi, rw_lo, router_b):
    n_p, d = x_p.shape
    n_s = x_s.shape[0]
    tm = min(OUT_TM, n_p, n_s)
    p_tiles, s_tiles = n_p // tm, n_s // tm
    n_all = n_p + n_s
    pidx = lambda i: (jnp.minimum(i, p_tiles - 1), 0)
    sidx = lambda i: (jnp.maximum(i - p_tiles, 0), 0)
    const = lambda i: (0, 0)
    return pl.pallas_call(
        functools.partial(_out_proj_kernel, p_tiles=p_tiles),
        out_shape=(jax.ShapeDtypeStruct((n_p, d), F32),
                   jax.ShapeDtypeStruct((n_s, d), F32),
                   jax.ShapeDtypeStruct((n_all, d), F32),
                   jax.ShapeDtypeStruct((n_all, N_EXPERTS), F32)),
        grid=(p_tiles + s_tiles,),
        in_specs=[pl.BlockSpec((tm, MIX_W), pidx),
                  pl.BlockSpec((tm, d), pidx),
                  pl.BlockSpec((tm, MIX_W), sidx),
                  pl.BlockSpec((tm, d), sidx),
                  pl.BlockSpec((MIX_W, d), const),
                  pl.BlockSpec((1, d), const),
                  pl.BlockSpec((d, N_EXPERTS), const),
                  pl.BlockSpec((d, N_EXPERTS), const),
                  pl.BlockSpec((1, N_EXPERTS), const)],
        out_specs=(pl.BlockSpec((tm, d), pidx),
                   pl.BlockSpec((tm, d), sidx),
                   pl.BlockSpec((tm, d), lambda i: (i, 0)),
                   pl.BlockSpec((tm, N_EXPERTS), lambda i: (i, 0))),
        compiler_params=_cparams(("arbitrary",)),
        name="out_proj",
    )(mix_p, x_p, mix_s, x_s, w_out, norm_w, rw_hi, rw_lo, router_b)


def _route(logits, bm):
    n = logits.shape[0]
    tk = n * TOP_K
    top_val, top_idx = lax.top_k(logits, TOP_K)
    weights = jax.nn.softmax(top_val, axis=-1)
    flat_e = top_idx.reshape(tk).astype(jnp.int32)
    onehot = (flat_e[:, None] == jnp.arange(N_EXPERTS, dtype=jnp.int32)[None, :]).astype(jnp.int32)
    csum = jnp.cumsum(onehot, axis=0)
    rank = jnp.sum((csum - onehot) * onehot, axis=1)
    counts = csum[-1]
    padded = (counts + bm - 1) // bm * bm
    pad_end = jnp.cumsum(padded)
    pad_start = pad_end - padded
    dest = jnp.sum(onehot * pad_start[None, :], axis=1) + rank
    n_blocks = -(-(tk + N_EXPERTS * (bm - 1)) // bm)
    n_rows = n_blocks * bm
    block_e = jnp.minimum(jnp.searchsorted(pad_end, jnp.arange(n_blocks) * bm, side='right'),
                          N_EXPERTS - 1).astype(jnp.int32)
    asg = jnp.arange(tk, dtype=jnp.int32)
    row_asg = jnp.full((n_rows,), -1, jnp.int32).at[dest].set(asg)
    is_pad = row_asg < 0
    row_tok = jnp.where(is_pad, 0, row_asg // TOP_K)
    row_slot = jnp.where(is_pad, tk + jnp.arange(n_rows, dtype=jnp.int32) % bm,
                         (row_asg % TOP_K) * n + row_asg // TOP_K)
    n_real = (pad_end[-1] // bm).astype(jnp.int32).reshape(1)
    return weights, row_tok, row_slot, block_e, n_real


def _row_copy(src_ref, dst_ref, sem, src_row, dst_row, n_rows=1):
    return pltpu.make_async_copy(src_ref.at[pl.ds(src_row, n_rows)],
                                 dst_ref.at[pl.ds(dst_row, n_rows)], sem)


def _moe_kernel(be_ref, nreal_ref, idx_ref, idx_next_ref, slot_prev_ref, x_hbm,
                gw_ref, gb_ref, uw_ref, ub_ref, dw_ref, db_ref, ys_hbm,
                acc_ref, xb_ref, xg_ref, yb_ref, gsem, ssem, *, bm, nf):
    blk = pl.program_id(0)
    f = pl.program_id(1)
    n_real = nreal_ref[0]

    def gather(rows_ref, i):
        return _row_copy(x_hbm, xg_ref, gsem, rows_ref[0, 0, i], i)

    def scatter(i):
        return _row_copy(yb_ref, ys_hbm, ssem, i, slot_prev_ref[0, 0, i])

    def wait_gather():
        _row_copy(x_hbm, xg_ref, gsem, 0, 0, bm).wait()

    def wait_scatter():
        _row_copy(yb_ref, ys_hbm, ssem, 0, 0, bm).wait()

    def start_all(make):
        def body(i, carry):
            make(i).start()
            return carry
        lax.fori_loop(0, bm, body, 0)

    def step(fs):
        if fs == 0:
            @pl.when(blk == 0)
            def _():
                yb_ref[...] = jnp.zeros_like(yb_ref)
                start_all(lambda i: gather(idx_ref, i))

            wait_gather()
            xb_ref[...] = xg_ref[...].astype(BF16)
            for i in range(bm):
                scatter(i).start()
        if fs == nf - 1:
            for i in range(bm):
                gather(idx_next_ref, i).start()

        xb = xb_ref[...]
        g = jnp.minimum(_dot(xb, gw_ref[0]) + gb_ref[0], SWIGLU_LIMIT)
        u = jnp.clip(_dot(xb, uw_ref[0]) + ub_ref[0], -SWIGLU_LIMIT, SWIGLU_LIMIT)
        hmid = g * jax.nn.sigmoid(SWIGLU_ALPHA * g) * (u + 1.0)
        part = _dot(hmid.astype(BF16), dw_ref[0])
        if fs > 0:
            part = acc_ref[...] + part
        if fs < nf - 1:
            acc_ref[...] = part
        else:
            wait_scatter()
            yb_ref[...] = part + db_ref[0]

    for fs in range(nf):
        pl.when(jnp.logical_and(blk < n_real, f == fs))(functools.partial(step, fs))

    @pl.when(jnp.logical_and(blk == n_real, f == 0))
    def _():
        wait_gather()
        start_all(scatter)
        wait_scatter()


def _moe_experts(x_all, row_tok, row_slot, block_e, n_real, gate_w, gate_b, up_w, up_b, down_w, down_b,
                 *, bm):
    n_rows = row_tok.shape[0]
    n, d = x_all.shape
    n_blocks = n_rows // bm
    d_ff = gate_w.shape[-1]
    tf = min(MOE_TF, d_ff)
    nf = d_ff // tf
    idx = row_tok.reshape(n_blocks, 1, bm)
    trash = TOP_K * n + jnp.arange(bm, dtype=jnp.int32)
    slot_prev = jnp.concatenate([trash, row_slot]).reshape(n_blocks + 1, 1, bm)

    def e_of(b, be, nr):
        return be[jnp.minimum(b, nr[0] - 1)]

    def f_of(b, f, nr):
        return jnp.where(b < nr[0], f, nf - 1)

    smem = functools.partial(pl.BlockSpec, (1, 1, bm), memory_space=pltpu.SMEM)
    return pl.pallas_call(
        functools.partial(_moe_kernel, bm=bm, nf=nf),
        out_shape=jax.ShapeDtypeStruct((TOP_K * n + bm, d), F32),
        grid_spec=pltpu.PrefetchScalarGridSpec(
            num_scalar_prefetch=2,
            grid=(n_blocks, nf),
            in_specs=[
                smem(lambda b, f, be, nr: (b, 0, 0)),
                smem(lambda b, f, be, nr: (jnp.minimum(b + 1, n_blocks - 1), 0, 0)),
                smem(lambda b, f, be, nr: (b, 0, 0)),
                pl.BlockSpec(memory_space=pl.ANY),
                pl.BlockSpec((1, d, tf), lambda b, f, be, nr: (e_of(b, be, nr), 0, f_of(b, f, nr))),
                pl.BlockSpec((1, 1, tf), lambda b, f, be, nr: (e_of(b, be, nr), 0, f_of(b, f, nr))),
                pl.BlockSpec((1, d, tf), lambda b, f, be, nr: (e_of(b, be, nr), 0, f_of(b, f, nr))),
                pl.BlockSpec((1, 1, tf), lambda b, f, be, nr: (e_of(b, be, nr), 0, f_of(b, f, nr))),
                pl.BlockSpec((1, tf, d), lambda b, f, be, nr: (e_of(b, be, nr), f_of(b, f, nr), 0)),
                pl.BlockSpec((1, 1, d), lambda b, f, be, nr: (e_of(b, be, nr), 0, 0)),
            ],
            out_specs=pl.BlockSpec(memory_space=pl.ANY),
            scratch_shapes=[pltpu.VMEM((bm, d), F32), pltpu.VMEM((bm, d), BF16),
                            pltpu.VMEM((bm, d), F32), pltpu.VMEM((bm, d), F32),
                            pltpu.SemaphoreType.DMA, pltpu.SemaphoreType.DMA]),
        compiler_params=_cparams(("arbitrary", "arbitrary"), MOE_VMEM_LIMIT),
        name="moe_experts",
    )(block_e, n_real, idx, idx, slot_prev, x_all, gate_w, gate_b, up_w, up_b, down_w, down_b)


def _combine_kernel(*refs):
    y_refs, (w_ref, h_ref, nw_ref, out_ref) = refs[:TOP_K], refs[TOP_K:]
    w = w_ref[...]
    y = w[:, 0:1] * y_refs[0][...]
    for k in range(1, TOP_K):
        y = y + w[:, k:k + 1] * y_refs[k][...]
    out_ref[...] = _rms(h_ref[...] + y) * nw_ref[...]


def _combine(y_slots, weights, h, final_norm_w, *, tok_off):
    n, d = h.shape
    n_all = weights.shape[0]
    tm = min(COMBINE_TM, n)
    ob = tok_off // tm
    slot_blocks = n_all // tm
    y_specs = [pl.BlockSpec((tm, d), functools.partial(lambda k, i: (k * slot_blocks + ob + i, 0), k))
               for k in range(TOP_K)]
    return pl.pallas_call(
        _combine_kernel,
        out_shape=jax.ShapeDtypeStruct((n, d), F32),
        grid=(n // tm,),
        in_specs=[*y_specs,
                  pl.BlockSpec((tm, TOP_K), lambda i: (i + ob, 0)),
                  pl.BlockSpec((tm, d), lambda i: (i, 0)),
                  pl.BlockSpec((1, d), lambda i: (0, 0))],
        out_specs=pl.BlockSpec((tm, d), lambda i: (i, 0)),
        compiler_params=_cparams(("parallel",)),
        name="moe_combine",
    )(*([y_slots] * TOP_K), weights, h, final_norm_w)


def _split_w_in(w_in):
    lr0 = 2 * QA_W + VA_W
    w_main = jnp.concatenate([w_in[:, :lr0], w_in[:, lr0 + GLA_GATE_RANK:]], axis=1).astype(BF16)
    w_lr = jnp.pad(w_in[:, lr0:lr0 + GLA_GATE_RANK], ((0, 0), (0, LR_PAD - GLA_GATE_RANK))).astype(BF16)
    return w_main, w_lr


def kernel(x_prompt, x_sample, state_gla, cache_swa_k, cache_swa_v, attn_norm_w, w_in, gla_gate_up_w, gla_gate_b, gla_out_norm_w, swa_sinks, w_out, ffn_norm_w, router_w, router_b, expert_gate_w, expert_gate_b, expert_up_w, expert_up_b, expert_down_w, expert_down_b, final_norm_w):
    depth = w_in.shape[0]
    assert depth == 1, "the combine applies the final norm, so exactly one layer is supported"
    bp, tp, d = x_prompt.shape
    bs, ts, _ = x_sample.shape
    n_p, n_s = bp * tp, bs * ts
    hp = x_prompt.reshape(n_p, d)
    hs = x_sample.reshape(n_s, d)
    l = 0
    w_main, w_lr = _split_w_in(w_in[l])
    gup = jnp.pad(gla_gate_up_w[l], ((0, LR_PAD - GLA_GATE_RANK), (0, 0))).astype(BF16)
    gate_b = gla_gate_b[l].reshape(1, QA_W)
    onw = gla_out_norm_w[l].reshape(1, GLA_DV)
    anw = attn_norm_w[l].reshape(1, d)
    rw_hi = router_w[l].astype(BF16)
    rw_lo = (router_w[l] - rw_hi.astype(F32)).astype(BF16)

    proj_p, lr_p, kv_p = _in_proj(hp, anw, w_main, w_lr)
    mix_p, st_p = _mix_prompt(proj_p, lr_p, gup, gate_b, onw, swa_sinks[l], batch=bp, seq=tp)
    proj_s, lr_s, kv_s = _in_proj(hs, anw, w_main, w_lr)
    mix_s, st_s = _mix_sample(proj_s, lr_s, gup, gate_b, onw, swa_sinks[l],
                              jnp.swapaxes(state_gla[l], -1, -2),
                              cache_swa_k[l].reshape(bs, WINDOW, KB_W),
                              cache_swa_v[l].reshape(bs, WINDOW, KB_W), batch=bs, seq=ts)

    hp, hs, xn_all, lg_all = _out_proj(mix_p, hp, mix_s, hs, w_out[l].astype(BF16),
                                       ffn_norm_w[l].reshape(1, d), rw_hi, rw_lo,
                                       router_b[l].reshape(1, N_EXPERTS))

    weights, row_tok, row_slot, block_e, n_real = _route(lg_all, MOE_BM)
    y_slots = _moe_experts(xn_all, row_tok, row_slot, block_e, n_real,
                          expert_gate_w[l].astype(BF16), expert_gate_b[l][:, None, :],
                          expert_up_w[l].astype(BF16), expert_up_b[l][:, None, :],
                          expert_down_w[l].astype(BF16), expert_down_b[l][:, None, :], bm=MOE_BM)
    fin = final_norm_w.reshape(1, d)
    y_prompt = _combine(y_slots, weights, hp, fin, tok_off=0).reshape(bp, tp, d)
    y_sample = _combine(y_slots, weights, hs, fin, tok_off=n_p).reshape(bs, ts, d)

    kvp = kv_p.reshape(bp, tp, 2, SWA_KV_HEADS, HEAD_DIM)[:, -WINDOW:]
    kvs = kv_s.reshape(bs, ts, 2, SWA_KV_HEADS, HEAD_DIM)
    return (y_prompt, y_sample,
            jnp.swapaxes(st_p, -1, -2)[None], kvp[:, :, 0][None], kvp[:, :, 1][None],
            jnp.swapaxes(st_s, -1, -2)[None],
            jnp.concatenate([cache_swa_k[l], kvs[:, :, 0]], axis=1)[:, -WINDOW:][None],
            jnp.concatenate([cache_swa_v[l], kvs[:, :, 1]], axis=1)[:, -WINDOW:][None])
```

```python
import functools

import jax
import jax.numpy as jnp
from jax import lax
from jax.experimental import pallas as pl
from jax.experimental.pallas import tpu as pltpu

F32 = jnp.float32
BF16 = jnp.bfloat16
U32 = jnp.uint32

CHUNK = 64
GLA_HEADS = 4
GLA_DK = 128
GLA_DV = 256
GLA_GATE_RANK = 16
GLA_GATE_NORM = 16.0
SWA_Q_HEADS = 16
SWA_KV_HEADS = 4
SWA_GROUP = SWA_Q_HEADS // SWA_KV_HEADS
HEAD_DIM = 64
WINDOW = 128
WINDOW_CHUNKS = WINDOW // CHUNK
ALIBI_MAX_BIAS = 8.0
N_EXPERTS = 32
TOP_K = 4
SWIGLU_LIMIT = 7.0
SWIGLU_ALPHA = 1.702
RMS_EPS = 1e-5
NEG_INF = -1e30

QA_W = GLA_HEADS * GLA_DK
VA_W = GLA_HEADS * GLA_DV
QB_W = SWA_Q_HEADS * HEAD_DIM
KB_W = SWA_KV_HEADS * HEAD_DIM
MIX_W = VA_W + QB_W
MAIN_W = 2 * QA_W + 2 * VA_W + QB_W + 2 * KB_W
QB_COL = (2 * QA_W + 2 * VA_W) // QB_W
KB_COL = (2 * QA_W + 2 * VA_W + QB_W) // KB_W
LANE = 128
LR_PAD = LANE

PROJ_TN = 1536
ROW_TILE = 512
MOE_BM = 512
MOE_TF = 1024
CAST_MAX_ROWS = 256
OUT_TM = 256
COMBINE_TM = 128
VMEM_LIMIT = 48 * 1024 * 1024
MOE_VMEM_LIMIT = 58 * 1024 * 1024


def _cparams(sem, vmem=VMEM_LIMIT):
    return pltpu.CompilerParams(dimension_semantics=sem, vmem_limit_bytes=vmem)


def _dot(a, b):
    return jnp.dot(a, b, preferred_element_type=F32)


def _dot_tn(a, b):
    return lax.dot_general(a, b, (((0,), (0,)), ((), ())), preferred_element_type=F32)


def _dot_nt(a, b):
    return lax.dot_general(a, b, (((1,), (1,)), ((), ())), preferred_element_type=F32)


def _rms(x):
    return x * lax.rsqrt(jnp.mean(x * x, axis=-1, keepdims=True) + RMS_EPS)


def _hi_lo(x):
    hi = x.astype(BF16)
    return hi, (x - hi.astype(F32)).astype(BF16)


def _pack_halves(x):
    c = x.shape[1] // 2
    hi = lax.bitcast_convert_type(x[:, :c].astype(BF16).astype(F32), U32)
    lo = lax.bitcast_convert_type(x[:, c:].astype(BF16).astype(F32), U32)
    return hi | (lo >> 16)


def _unpack_halves(w):
    hi = lax.bitcast_convert_type(w & jnp.uint32(0xFFFF0000), F32)
    lo = lax.bitcast_convert_type(w << 16, F32)
    return hi, lo


def _in_proj_kernel(x_ref, nw_ref, w_ref, wlr_ref, out_ref, lr_ref, kv_ref, xn_ref, *, n_col_steps):
    j = pl.program_id(1)

    @pl.when(j == 0)
    def _():
        xnb = (_rms(x_ref[...]) * nw_ref[...]).astype(BF16)
        xn_ref[...] = xnb
        lr_ref[...] = _dot(xnb, wlr_ref[...]).astype(BF16)

    acc = _dot(xn_ref[...], w_ref[...])
    out_ref[...] = acc.astype(BF16)

    @pl.when(j == n_col_steps - 1)
    def _():
        kv_ref[...] = acc[:, PROJ_TN - 2 * KB_W:]


def _in_proj(x2d, norm_w, w_main, w_lr):
    n, d = x2d.shape
    tm = min(ROW_TILE, n)
    n_col_steps = MAIN_W // PROJ_TN
    return pl.pallas_call(
        functools.partial(_in_proj_kernel, n_col_steps=n_col_steps),
        out_shape=(jax.ShapeDtypeStruct((n, MAIN_W), BF16),
                   jax.ShapeDtypeStruct((n, LR_PAD), BF16),
                   jax.ShapeDtypeStruct((n, 2 * KB_W), F32)),
        grid=(n // tm, n_col_steps),
        in_specs=[pl.BlockSpec((tm, d), lambda i, j: (i, 0)),
                  pl.BlockSpec((1, d), lambda i, j: (0, 0)),
                  pl.BlockSpec((d, PROJ_TN), lambda i, j: (0, j)),
                  pl.BlockSpec((d, LR_PAD), lambda i, j: (0, 0))],
        out_specs=(pl.BlockSpec((tm, PROJ_TN), lambda i, j: (i, j)),
                   pl.BlockSpec((tm, LR_PAD), lambda i, j: (i, 0)),
                   pl.BlockSpec((tm, 2 * KB_W), lambda i, j: (i, 0))),
        scratch_shapes=[pltpu.VMEM((tm, d), BF16)],
        compiler_params=_cparams(("parallel", "arbitrary")),
        name="in_proj",
    )(x2d, norm_w, w_main, w_lr)


def _gla_chunk(rows, chunk, q_ref, k_ref, v_ref, ga_ref, lr_ref, gup, gb, nw, tri, s_scr, o_ref):
    z = _dot(lr_ref[rows, :], gup) + gb
    fc = (jnp.minimum(z, 0.0) - jnp.log1p(jnp.exp(-jnp.abs(z)))) * (1.0 / GLA_GATE_NORM)
    hi, lo = _hi_lo(fc)
    g = _dot(tri, hi) + _dot(tri, lo)
    g_end = g[chunk - 1:chunk, :]
    k_dec = (k_ref[rows, :].astype(F32) * jnp.exp(g_end - g)).astype(BF16)
    a_end = jnp.exp(g_end)
    for h in range(GLA_HEADS):
        ks = slice(h * GLA_DK, (h + 1) * GLA_DK)
        vs = slice(h * GLA_DV, (h + 1) * GLA_DV)
        upd = _dot_tn(v_ref[rows, vs], k_dec[:, ks])
        s_new = s_scr[h] * a_end[:, ks] + upd
        s_scr[h] = s_new
        o = _dot_nt(q_ref[rows, ks], s_new.astype(BF16)) * (GLA_DK ** -0.5)
        gate = ga_ref[rows, vs].astype(F32)
        o_ref[rows, vs] = (_rms(o) * nw * (gate * jax.nn.sigmoid(gate))).astype(BF16)


def _alibi_slopes():
    s = jnp.exp2(-ALIBI_MAX_BIAS * jnp.arange(1, SWA_Q_HEADS + 1, dtype=F32) / SWA_Q_HEADS)
    return s.reshape(SWA_KV_HEADS, SWA_GROUP)


def _stacked_bias(q_off, lq, lk):
    dist = jnp.abs(q_off + jnp.arange(lq)[:, None] - jnp.arange(lk)[None, :]).astype(F32)
    b = -_alibi_slopes()[:, :, None, None] * dist
    return b.reshape(SWA_KV_HEADS, SWA_GROUP * lq, lk)


def _stacked_sinks(sinks, lq):
    s = sinks.astype(F32).reshape(SWA_KV_HEADS, SWA_GROUP, 1)
    return jnp.broadcast_to(s, (SWA_KV_HEADS, SWA_GROUP, lq)).reshape(SWA_KV_HEADS, SWA_GROUP * lq, 1)


def _attend_group(q, keys, vals, bias, sink, valid):
    lq = q.shape[0]
    qs = jnp.concatenate([q[:, j * HEAD_DIM:(j + 1) * HEAD_DIM] for j in range(SWA_GROUP)], axis=0)
    s = _dot_nt(qs, keys) * (HEAD_DIM ** -0.5) + bias
    if valid is not None:
        s = jnp.where(valid, s, NEG_INF)
    m = jnp.maximum(jnp.max(s, axis=-1, keepdims=True), sink)
    p = jnp.exp(s - m)
    denom = jnp.sum(p, axis=-1, keepdims=True) + jnp.exp(sink - m)
    o = _dot(p.astype(BF16), vals) / denom
    return jnp.concatenate([o[j * lq:(j + 1) * lq, :] for j in range(SWA_GROUP)], axis=1)


def _mix_prompt_kernel(*refs, n_chunks, n_steps, n_cast, cast_rows):
    (q_ref, k_ref, v_ref, ga_ref, lr_ref, qb_ref, kc_ref, vc_ref, kp_ref, vp_ref,
     gup_ref, gb_ref, nw_ref, tri_ref, bias_ref, sink_ref) = refs[:16]
    w_hbm = refs[16:16 + n_cast]
    o_ref, st_ref = refs[16 + n_cast:18 + n_cast]
    wb_hbm = refs[18 + n_cast:18 + 2 * n_cast]
    s_scr, k_scr, v_scr = refs[18 + 2 * n_cast:21 + 2 * n_cast]
    if n_cast:
        win, wout, isem, osem = refs[21 + 2 * n_cast:]
    t = pl.program_id(1)
    back = WINDOW_CHUNKS * CHUNK
    span = back + CHUNK
    pairs = n_chunks // 2
    step_id = pl.program_id(0) * pl.num_programs(1) + t
    total = n_steps * pairs

    def w_in(j, piece, slot):
        return pltpu.make_async_copy(w_hbm[j].at[pl.ds(piece * cast_rows, cast_rows)],
                                     win.at[slot, j], isem.at[slot])

    def w_out(j, piece, slot):
        return pltpu.make_async_copy(wout.at[slot, j],
                                     wb_hbm[j].at[pl.ds(piece * cast_rows, cast_rows)], osem.at[slot])

    @pl.when(t == 0)
    def _():
        s_scr[...] = jnp.zeros_like(s_scr)

    k_scr[0:back, :] = kp_ref[...]
    v_scr[0:back, :] = vp_ref[...]
    k_scr[back:, :] = kc_ref[...]
    v_scr[back:, :] = vc_ref[...]
    key_idx = lax.broadcasted_iota(jnp.int32, (SWA_GROUP * CHUNK, span), 1)
    gup, gb, nw, tri = gup_ref[...], gb_ref[...], nw_ref[...], tri_ref[...]

    def chunk_step(c):
        r0 = pl.multiple_of(c * CHUNK, CHUNK)
        rows = pl.ds(r0, CHUNK)
        _gla_chunk(rows, CHUNK, q_ref, k_ref, v_ref, ga_ref, lr_ref, gup, gb, nw, tri, s_scr, o_ref)
        valid = jnp.logical_or(t > 0, key_idx >= back - c * CHUNK)
        for g in range(SWA_KV_HEADS):
            hs = slice(g * HEAD_DIM, (g + 1) * HEAD_DIM)
            qs = slice(g * SWA_GROUP * HEAD_DIM, (g + 1) * SWA_GROUP * HEAD_DIM)
            o = _attend_group(qb_ref[rows, qs], k_scr[pl.ds(r0, span), hs],
                              v_scr[pl.ds(r0, span), hs], bias_ref[g], sink_ref[g], valid)
            o_ref[rows, VA_W + g * SWA_GROUP * HEAD_DIM:VA_W + (g + 1) * SWA_GROUP * HEAD_DIM] = o.astype(BF16)

    def pair_step(p, carry):
        piece = step_id * pairs + p
        slot = piece % 2
        if n_cast:
            @pl.when(piece == 0)
            def _():
                for j in range(n_cast):
                    w_in(j, 0, 0).start()

            for j in range(n_cast):
                w_in(j, piece, slot).wait()

            @pl.when(piece + 1 < total)
            def _():
                for j in range(n_cast):
                    w_in(j, piece + 1, 1 - slot).start()

            @pl.when(piece >= 2)
            def _():
                for j in range(n_cast):
                    w_out(j, piece - 2, slot).wait()

        chunk_step(2 * p)
        chunk_step(2 * p + 1)
        if n_cast:
            for j in range(n_cast):
                wout[slot, j] = win[slot, j].astype(BF16)
            for j in range(n_cast):
                w_out(j, piece, slot).start()
        return carry

    lax.fori_loop(0, pairs, pair_step, 0)

    @pl.when(t == pl.num_programs(1) - 1)
    def _():
        st_ref[0] = s_scr[...]

    if n_cast:
        @pl.when(step_id == n_steps - 1)
        def _():
            for piece in (total - 2, total - 1):
                for j in range(n_cast):
                    w_out(j, piece, piece % 2).wait()


def _gla_consts(gup, gate_b, norm_w, chunk):
    tri = (jnp.arange(chunk)[:, None] >= jnp.arange(chunk)[None, :]).astype(BF16)
    specs = [pl.BlockSpec((LR_PAD, QA_W), lambda *_: (0, 0)),
             pl.BlockSpec((1, QA_W), lambda *_: (0, 0)),
             pl.BlockSpec((1, GLA_DV), lambda *_: (0, 0)),
             pl.BlockSpec((chunk, chunk), lambda *_: (0, 0))]
    return specs, [gup, gate_b, norm_w, tri]


def _mix_prompt(proj, lr, gup, gate_b, norm_w, sinks, cast_ws=(), *, batch, seq):
    rows = min(ROW_TILE, seq)
    n_chunks = rows // CHUNK
    nt = seq // rows
    back = WINDOW_CHUNKS * CHUNK
    pb = rows // back
    bias = _stacked_bias(back, CHUNK, back + CHUNK)
    sink = _stacked_sinks(sinks, CHUNK)
    rb = lambda b, t: b * nt + t
    prev = lambda b, t: jnp.maximum(rb(b, t) * pb - 1, 0)
    cspecs, cargs = _gla_consts(gup, gate_b, norm_w, CHUNK)
    n_cast = len(cast_ws)
    cast_rows, cast_scratch = 0, []
    if n_cast:
        assert n_chunks % 2 == 0
        w_rows, w_cols = cast_ws[0].shape
        pieces = batch * nt * (n_chunks // 2)
        assert all(w.shape == (w_rows, w_cols) for w in cast_ws) and w_rows % pieces == 0 and pieces >= 2
        cast_rows = w_rows // pieces
        cast_scratch = [pltpu.VMEM((2, n_cast, cast_rows, w_cols), F32),
                        pltpu.VMEM((2, n_cast, cast_rows, w_cols), BF16),
                        pltpu.SemaphoreType.DMA((2,)), pltpu.SemaphoreType.DMA((2,))]
    any_spec = pl.BlockSpec(memory_space=pl.ANY)
    return pl.pallas_call(
        functools.partial(_mix_prompt_kernel, n_chunks=n_chunks, n_steps=batch * nt, n_cast=n_cast,
                          cast_rows=cast_rows),
        out_shape=(jax.ShapeDtypeStruct((batch * seq, MIX_W), BF16),
                   jax.ShapeDtypeStruct((batch, GLA_HEADS, GLA_DV, GLA_DK), F32),
                   *[jax.ShapeDtypeStruct(w.shape, BF16) for w in cast_ws]),
        grid=(batch, nt),
        in_specs=[pl.BlockSpec((rows, QA_W), lambda b, t: (rb(b, t), 0)),
                  pl.BlockSpec((rows, QA_W), lambda b, t: (rb(b, t), 1)),
                  pl.BlockSpec((rows, VA_W), lambda b, t: (rb(b, t), 1)),
                  pl.BlockSpec((rows, VA_W), lambda b, t: (rb(b, t), 2)),
                  pl.BlockSpec((rows, LR_PAD), lambda b, t: (rb(b, t), 0)),
                  pl.BlockSpec((rows, QB_W), lambda b, t: (rb(b, t), QB_COL)),
                  pl.BlockSpec((rows, KB_W), lambda b, t: (rb(b, t), KB_COL)),
                  pl.BlockSpec((rows, KB_W), lambda b, t: (rb(b, t), KB_COL + 1)),
                  pl.BlockSpec((back, KB_W), lambda b, t: (prev(b, t), KB_COL)),
                  pl.BlockSpec((back, KB_W), lambda b, t: (prev(b, t), KB_COL + 1)),
                  *cspecs,
                  pl.BlockSpec(bias.shape, lambda b, t: (0, 0, 0)),
                  pl.BlockSpec(sink.shape, lambda b, t: (0, 0, 0)),
                  *[any_spec] * n_cast],
        out_specs=(pl.BlockSpec((rows, MIX_W), lambda b, t: (rb(b, t), 0)),
                   pl.BlockSpec((1, GLA_HEADS, GLA_DV, GLA_DK), lambda b, t: (b, 0, 0, 0)),
                   *[any_spec] * n_cast),
        scratch_shapes=[pltpu.VMEM((GLA_HEADS, GLA_DV, GLA_DK), F32),
                        pltpu.VMEM((rows + back, KB_W), BF16),
                        pltpu.VMEM((rows + back, KB_W), BF16),
                        *cast_scratch],
        compiler_params=_cparams(("arbitrary", "arbitrary")),
        name="mix_prompt",
    )(proj, proj, proj, proj, lr, proj, proj, proj, proj, proj, *cargs, bias, sink, *cast_ws)


def _mix_sample_kernel(q_ref, k_ref, v_ref, ga_ref, lr_ref, qb_ref, kn_ref, vn_ref, wk_ref, wv_ref,
                       s0_ref, gup_ref, gb_ref, nw_ref, tri_ref, bias_ref, sink_ref,
                       o_ref, st_ref, s_scr, *, seq):
    s_scr[...] = s0_ref[0]
    _gla_chunk(slice(0, seq), seq, q_ref, k_ref, v_ref, ga_ref, lr_ref,
               gup_ref[...], gb_ref[...], nw_ref[...], tri_ref[...], s_scr, o_ref)
    st_ref[0] = s_scr[...]
    keys = jnp.concatenate([wk_ref[0].astype(BF16), kn_ref[...]], axis=0)
    vals = jnp.concatenate([wv_ref[0].astype(BF16), vn_ref[...]], axis=0)
    for g in range(SWA_KV_HEADS):
        hs = slice(g * HEAD_DIM, (g + 1) * HEAD_DIM)
        qs = slice(g * SWA_GROUP * HEAD_DIM, (g + 1) * SWA_GROUP * HEAD_DIM)
        o = _attend_group(qb_ref[:, qs], keys[:, hs], vals[:, hs], bias_ref[g], sink_ref[g], None)
        o_ref[:, VA_W + g * SWA_GROUP * HEAD_DIM:VA_W + (g + 1) * SWA_GROUP * HEAD_DIM] = o.astype(BF16)


def _mix_sample(proj, lr, gup, gate_b, norm_w, sinks, s0_t, win_k, win_v, *, batch, seq):
    bias = _stacked_bias(WINDOW, seq, WINDOW + seq)
    sink = _stacked_sinks(sinks, seq)
    cspecs, cargs = _gla_consts(gup, gate_b, norm_w, seq)
    state_spec = pl.BlockSpec((1, GLA_HEADS, GLA_DV, GLA_DK), lambda b: (b, 0, 0, 0))
    return pl.pallas_call(
        functools.partial(_mix_sample_kernel, seq=seq),
        out_shape=(jax.ShapeDtypeStruct((batch * seq, MIX_W), BF16),
                   jax.ShapeDtypeStruct((batch, GLA_HEADS, GLA_DV, GLA_DK), F32)),
        grid=(batch,),
        in_specs=[pl.BlockSpec((seq, QA_W), lambda b: (b, 0)),
                  pl.BlockSpec((seq, QA_W), lambda b: (b, 1)),
                  pl.BlockSpec((seq, VA_W), lambda b: (b, 1)),
                  pl.BlockSpec((seq, VA_W), lambda b: (b, 2)),
                  pl.BlockSpec((seq, LR_PAD), lambda b: (b, 0)),
                  pl.BlockSpec((seq, QB_W), lambda b: (b, QB_COL)),
                  pl.BlockSpec((seq, KB_W), lambda b: (b, KB_COL)),
                  pl.BlockSpec((seq, KB_W), lambda b: (b, KB_COL + 1)),
                  pl.BlockSpec((1, WINDOW, KB_W), lambda b: (b, 0, 0)),
                  pl.BlockSpec((1, WINDOW, KB_W), lambda b: (b, 0, 0)),
                  state_spec,
                  *cspecs,
                  pl.BlockSpec(bias.shape, lambda b: (0, 0, 0)),
                  pl.BlockSpec(sink.shape, lambda b: (0, 0, 0))],
        out_specs=(pl.BlockSpec((seq, MIX_W), lambda b: (b, 0)), state_spec),
        scratch_shapes=[pltpu.VMEM((GLA_HEADS, GLA_DV, GLA_DK), F32)],
        compiler_params=_cparams(("parallel",)),
        name="mix_sample",
    )(proj, proj, proj, proj, lr, proj, proj, proj, win_k, win_v, s0_t, *cargs, bias, sink)


def _out_proj_kernel(mp_ref, xp_ref, ms_ref, xs_ref, w_ref, nw_ref, rwh_ref, rwl_ref, rb_ref,
                     hp_ref, hs_ref, xn_ref, lg_ref, *, p_tiles):
    i = pl.program_id(0)

    def tile(m_ref, x_ref, h_ref):
        h = x_ref[...] + _dot(m_ref[...], w_ref[...])
        h_ref[...] = h
        xn = _rms(h) * nw_ref[...]
        xn_ref[...] = _pack_halves(xn)
        xh, xl = _hi_lo(xn)
        rwh = rwh_ref[...]
        lg_ref[...] = _dot(xh, rwh) + _dot(xl, rwh) + _dot(xh, rwl_ref[...]) + rb_ref[...]

    @pl.when(i < p_tiles)
    def _():
        tile(mp_ref, xp_ref, hp_ref)

    @pl.when(i >= p_tiles)
    def _():
        tile(ms_ref, xs_ref, hs_ref)


def _out_proj(mix_p, x_p, mix_s, x_s, w_out, norm_w, rw_hi, rw_lo, router_b):
    n_p, d = x_p.shape
    n_s = x_s.shape[0]
    tm = min(OUT_TM, n_p, n_s)
    p_tiles, s_tiles = n_p // tm, n_s // tm
    n_all = n_p + n_s
    pidx = lambda i: (jnp.minimum(i, p_tiles - 1), 0)
    sidx = lambda i: (jnp.maximum(i - p_tiles, 0), 0)
    const = lambda i: (0, 0)
    return pl.pallas_call(
        functools.partial(_out_proj_kernel, p_tiles=p_tiles),
        out_shape=(jax.ShapeDtypeStruct((n_p, d), F32),
                   jax.ShapeDtypeStruct((n_s, d), F32),
                   jax.ShapeDtypeStruct((n_all, d // 2), U32),
                   jax.ShapeDtypeStruct((n_all, N_EXPERTS), F32)),
        grid=(p_tiles + s_tiles,),
        in_specs=[pl.BlockSpec((tm, MIX_W), pidx),
                  pl.BlockSpec((tm, d), pidx),
                  pl.BlockSpec((tm, MIX_W), sidx),
                  pl.BlockSpec((tm, d), sidx),
                  pl.BlockSpec((MIX_W, d), const),
                  pl.BlockSpec((1, d), const),
                  pl.BlockSpec((d, N_EXPERTS), const),
                  pl.BlockSpec((d, N_EXPERTS), const),
                  pl.BlockSpec((1, N_EXPERTS), const)],
        out_specs=(pl.BlockSpec((tm, d), pidx),
                   pl.BlockSpec((tm, d), sidx),
                   pl.BlockSpec((tm, d // 2), lambda i: (i, 0)),
                   pl.BlockSpec((tm, N_EXPERTS), lambda i: (i, 0))),
        compiler_params=_cparams(("arbitrary",)),
        name="out_proj",
    )(mix_p, x_p, mix_s, x_s, w_out, norm_w, rw_hi, rw_lo, router_b)


def _route(logits, bm):
    n = logits.shape[0]
    tk = n * TOP_K
    top_val, top_idx = lax.top_k(logits, TOP_K)
    weights = jax.nn.softmax(top_val, axis=-1)
    flat_e = top_idx.reshape(tk).astype(jnp.int32)
    onehot = (flat_e[:, None] == jnp.arange(N_EXPERTS, dtype=jnp.int32)[None, :]).astype(jnp.int32)
    csum = jnp.cumsum(onehot, axis=0)
    rank = jnp.sum((csum - onehot) * onehot, axis=1)
    counts = csum[-1]
    padded = (counts + bm - 1) // bm * bm
    pad_end = jnp.cumsum(padded)
    pad_start = pad_end - padded
    dest = jnp.sum(onehot * pad_start[None, :], axis=1) + rank
    n_blocks = -(-(tk + N_EXPERTS * (bm - 1)) // bm)
    n_rows = n_blocks * bm
    block_e = jnp.minimum(jnp.searchsorted(pad_end, jnp.arange(n_blocks) * bm, side='right'),
                          N_EXPERTS - 1).astype(jnp.int32)
    asg = jnp.arange(tk, dtype=jnp.int32)
    row_asg = jnp.full((n_rows,), -1, jnp.int32).at[dest].set(asg)
    is_pad = row_asg < 0
    row_tok = jnp.where(is_pad, 0, row_asg // TOP_K)
    row_slot = jnp.where(is_pad, tk + jnp.arange(n_rows, dtype=jnp.int32) % bm,
                         (row_asg % TOP_K) * n + row_asg // TOP_K)
    n_real = (pad_end[-1] // bm).astype(jnp.int32).reshape(1)
    return weights, row_tok, row_slot, block_e, n_real


def _row_copy(src_ref, dst_ref, sem, src_row, dst_row, n_rows=1):
    return pltpu.make_async_copy(src_ref.at[pl.ds(src_row, n_rows)],
                                 dst_ref.at[pl.ds(dst_row, n_rows)], sem)


def _moe_kernel(be_ref, nreal_ref, idx_ref, idx_next_ref, slot_prev_ref, x_hbm,
                gw_ref, gb_ref, uw_ref, ub_ref, dw_ref, db_ref, ys_hbm,
                acc_ref, xb_ref, xg_ref, yb_ref, gsem, ssem, *, bm, nf):
    blk = pl.program_id(0)
    f = pl.program_id(1)
    n_real = nreal_ref[0]
    cur = blk % 2

    def gather(rows_ref, i):
        return _row_copy(x_hbm, xg_ref, gsem, rows_ref[0, 0, i], i)

    def scatter(i):
        return _row_copy(yb_ref, ys_hbm, ssem, i, slot_prev_ref[0, 0, i])

    def wait_gather():
        _row_copy(x_hbm, xg_ref, gsem, 0, 0, bm).wait()

    def wait_scatter():
        _row_copy(yb_ref, ys_hbm, ssem, 0, 0, bm).wait()

    def start_all(make):
        def body(i, carry):
            make(i).start()
            return carry
        lax.fori_loop(0, bm, body, 0)

    def unpack_rows(slot):
        hi, lo = _unpack_halves(xg_ref[...])
        xb_ref[slot] = jnp.concatenate([hi.astype(BF16), lo.astype(BF16)], axis=1)

    def step(fs):
        if fs == 0:
            @pl.when(blk == 0)
            def _():
                yb_ref[...] = jnp.zeros_like(yb_ref)
                start_all(lambda i: gather(idx_ref, i))
                wait_gather()
                unpack_rows(0)

            for i in range(bm):
                gather(idx_next_ref, i).start()
        if fs == nf - 1:
            wait_gather()
            for i in range(bm):
                scatter(i).start()
            unpack_rows(1 - cur)

        xb = xb_ref[cur]
        g = jnp.minimum(_dot(xb, gw_ref[0]) + gb_ref[0], SWIGLU_LIMIT)
        u = jnp.clip(_dot(xb, uw_ref[0]) + ub_ref[0], -SWIGLU_LIMIT, SWIGLU_LIMIT)
        hmid = g * jax.nn.sigmoid(SWIGLU_ALPHA * g) * (u + 1.0)
        part = _dot(hmid.astype(BF16), dw_ref[0])
        if fs > 0:
            part = acc_ref[...] + part
        if fs < nf - 1:
            acc_ref[...] = part
        else:
            wait_scatter()
            yb_ref[...] = _pack_halves(part + db_ref[0])

    for fs in range(nf):
        pl.when(jnp.logical_and(blk < n_real, f == fs))(functools.partial(step, fs))

    @pl.when(jnp.logical_and(blk == n_real, f == 0))
    def _():
        start_all(scatter)
        wait_scatter()


def _moe_experts(x_all, row_tok, row_slot, block_e, n_real, gate_w, gate_b, up_w, up_b, down_w, down_b,
                 *, bm):
    n_rows = row_tok.shape[0]
    n, dw = x_all.shape
    d = 2 * dw
    n_blocks = n_rows // bm
    d_ff = gate_w.shape[-1]
    tf = min(MOE_TF, d_ff)
    nf = d_ff // tf
    assert nf >= 2, "the gather started in the first d_ff step is drained in the last one"
    idx = row_tok.reshape(n_blocks, 1, bm)
    trash = TOP_K * n + jnp.arange(bm, dtype=jnp.int32)
    slot_prev = jnp.concatenate([trash, row_slot]).reshape(n_blocks + 1, 1, bm)

    def e_of(b, be, nr):
        return be[jnp.minimum(b, nr[0] - 1)]

    def f_of(b, f, nr):
        return jnp.where(b < nr[0], f, nf - 1)

    smem = functools.partial(pl.BlockSpec, (1, 1, bm), memory_space=pltpu.SMEM)
    return pl.pallas_call(
        functools.partial(_moe_kernel, bm=bm, nf=nf),
        out_shape=jax.ShapeDtypeStruct((TOP_K * n + bm, dw), U32),
        grid_spec=pltpu.PrefetchScalarGridSpec(
            num_scalar_prefetch=2,
            grid=(n_blocks, nf),
            in_specs=[
                smem(lambda b, f, be, nr: (b, 0, 0)),
                smem(lambda b, f, be, nr: (jnp.minimum(b + 1, n_blocks - 1), 0, 0)),
                smem(lambda b, f, be, nr: (b, 0, 0)),
                pl.BlockSpec(memory_space=pl.ANY),
                pl.BlockSpec((1, d, tf), lambda b, f, be, nr: (e_of(b, be, nr), 0, f_of(b, f, nr))),
                pl.BlockSpec((1, 1, tf), lambda b, f, be, nr: (e_of(b, be, nr), 0, f_of(b, f, nr))),
                pl.BlockSpec((1, d, tf), lambda b, f, be, nr: (e_of(b, be, nr), 0, f_of(b, f, nr))),
                pl.BlockSpec((1, 1, tf), lambda b, f, be, nr: (e_of(b, be, nr), 0, f_of(b, f, nr))),
                pl.BlockSpec((1, tf, d), lambda b, f, be, nr: (e_of(b, be, nr), f_of(b, f, nr), 0)),
                pl.BlockSpec((1, 1, d), lambda b, f, be, nr: (e_of(b, be, nr), 0, 0)),
            ],
            out_specs=pl.BlockSpec(memory_space=pl.ANY),
            scratch_shapes=[pltpu.VMEM((bm, d), F32), pltpu.VMEM((2, bm, d), BF16),
                            pltpu.VMEM((bm, dw), U32), pltpu.VMEM((bm, dw), U32),
                            pltpu.SemaphoreType.DMA, pltpu.SemaphoreType.DMA]),
        compiler_params=_cparams(("arbitrary", "arbitrary"), MOE_VMEM_LIMIT),
        name="moe_experts",
    )(block_e, n_real, idx, idx, slot_prev, x_all, gate_w, gate_b, up_w, up_b, down_w, down_b)


def _combine_kernel(*refs):
    y_refs, (w_ref, h_ref, nw_ref, out_ref) = refs[:TOP_K], refs[TOP_K:]
    w = w_ref[...]
    c = out_ref.shape[1] // 2
    left, right = h_ref[:, :c], h_ref[:, c:]
    for k in range(TOP_K):
        hi, lo = _unpack_halves(y_refs[k][...])
        left = left + w[:, k:k + 1] * hi
        right = right + w[:, k:k + 1] * lo
    ms = jnp.sum(left * left, axis=-1, keepdims=True) + jnp.sum(right * right, axis=-1, keepdims=True)
    scale = lax.rsqrt(ms / (2 * c) + RMS_EPS)
    out_ref[:, :c] = left * scale * nw_ref[:, :c]
    out_ref[:, c:] = right * scale * nw_ref[:, c:]


def _combine(y_slots, weights, h, final_norm_w, *, tok_off):
    n, d = h.shape
    n_all = weights.shape[0]
    tm = min(COMBINE_TM, n)
    ob = tok_off // tm
    slot_blocks = n_all // tm
    y_specs = [pl.BlockSpec((tm, d // 2), functools.partial(lambda k, i: (k * slot_blocks + ob + i, 0), k))
               for k in range(TOP_K)]
    return pl.pallas_call(
        _combine_kernel,
        out_shape=jax.ShapeDtypeStruct((n, d), F32),
        grid=(n // tm,),
        in_specs=[*y_specs,
                  pl.BlockSpec((tm, TOP_K), lambda i: (i + ob, 0)),
                  pl.BlockSpec((tm, d), lambda i: (i, 0)),
                  pl.BlockSpec((1, d), lambda i: (0, 0))],
        out_specs=pl.BlockSpec((tm, d), lambda i: (i, 0)),
        compiler_params=_cparams(("parallel",)),
        name="moe_combine",
    )(*([y_slots] * TOP_K), weights, h, final_norm_w)


def _split_w_in(w_in):
    lr0 = 2 * QA_W + VA_W
    w_main = jnp.concatenate([w_in[:, :lr0], w_in[:, lr0 + GLA_GATE_RANK:]], axis=1).astype(BF16)
    w_lr = jnp.pad(w_in[:, lr0:lr0 + GLA_GATE_RANK], ((0, 0), (0, LR_PAD - GLA_GATE_RANK))).astype(BF16)
    return w_main, w_lr


def kernel(x_prompt, x_sample, state_gla, cache_swa_k, cache_swa_v, attn_norm_w, w_in, gla_gate_up_w, gla_gate_b, gla_out_norm_w, swa_sinks, w_out, ffn_norm_w, router_w, router_b, expert_gate_w, expert_gate_b, expert_up_w, expert_up_b, expert_down_w, expert_down_b, final_norm_w):
    depth = w_in.shape[0]
    assert depth == 1, "the combine applies the final norm, so exactly one layer is supported"
    bp, tp, d = x_prompt.shape
    bs, ts, _ = x_sample.shape
    n_p, n_s = bp * tp, bs * ts
    hp = x_prompt.reshape(n_p, d)
    hs = x_sample.reshape(n_s, d)
    l = 0
    w_main, w_lr = _split_w_in(w_in[l])
    gup = jnp.pad(gla_gate_up_w[l], ((0, LR_PAD - GLA_GATE_RANK), (0, 0))).astype(BF16)
    gate_b = gla_gate_b[l].reshape(1, QA_W)
    onw = gla_out_norm_w[l].reshape(1, GLA_DV)
    anw = attn_norm_w[l].reshape(1, d)
    rw_hi = router_w[l].astype(BF16)
    rw_lo = (router_w[l] - rw_hi.astype(F32)).astype(BF16)

    proj_p, lr_p, kv_p = _in_proj(hp, anw, w_main, w_lr)
    expert_ws = (expert_gate_w[l], expert_up_w[l], expert_down_w[l])
    flat_ws = tuple(w.reshape(-1, w.shape[-1]) for w in expert_ws)
    pieces = n_p // (2 * CHUNK)
    on_side = (len({w.shape for w in flat_ws}) == 1 and pieces >= 2 and tp % (2 * CHUNK) == 0
               and flat_ws[0].shape[0] % pieces == 0 and flat_ws[0].shape[0] // pieces <= CAST_MAX_ROWS)
    mix_p, st_p, *flat_wb = _mix_prompt(proj_p, lr_p, gup, gate_b, onw, swa_sinks[l],
                                        flat_ws if on_side else (), batch=bp, seq=tp)
    if not on_side:
        flat_wb = [w.astype(BF16) for w in flat_ws]
    gate_w16, up_w16, down_w16 = (wb.reshape(w.shape) for wb, w in zip(flat_wb, expert_ws))
    proj_s, lr_s, kv_s = _in_proj(hs, anw, w_main, w_lr)
    mix_s, st_s = _mix_sample(proj_s, lr_s, gup, gate_b, onw, swa_sinks[l],
                              jnp.swapaxes(state_gla[l], -1, -2),
                              cache_swa_k[l].reshape(bs, WINDOW, KB_W),
                              cache_swa_v[l].reshape(bs, WINDOW, KB_W), batch=bs, seq=ts)

    hp, hs, xn_all, lg_all = _out_proj(mix_p, hp, mix_s, hs, w_out[l].astype(BF16),
                                       ffn_norm_w[l].reshape(1, d), rw_hi, rw_lo,
                                       router_b[l].reshape(1, N_EXPERTS))

    weights, row_tok, row_slot, block_e, n_real = _route(lg_all, MOE_BM)
    y_slots = _moe_experts(xn_all, row_tok, row_slot, block_e, n_real,
                          gate_w16, expert_gate_b[l][:, None, :], up_w16, expert_up_b[l][:, None, :],
                          down_w16, expert_down_b[l][:, None, :], bm=MOE_BM)
    fin = final_norm_w.reshape(1, d)
    y_prompt = _combine(y_slots, weights, hp, fin, tok_off=0).reshape(bp, tp, d)
    y_sample = _combine(y_slots, weights, hs, fin, tok_off=n_p).reshape(bs, ts, d)

    kvp = kv_p.reshape(bp, tp, 2, SWA_KV_HEADS, HEAD_DIM)[:, -WINDOW:]
    kvs = kv_s.reshape(bs, ts, 2, SWA_KV_HEADS, HEAD_DIM)
    return (y_prompt, y_sample,
            jnp.swapaxes(st_p, -1, -2)[None], kvp[:, :, 0][None], kvp[:, :, 1][None],
            jnp.swapaxes(st_s, -1, -2)[None],
            jnp.concatenate([cache_swa_k[l], kvs[:, :, 0]], axis=1)[:, -WINDOW:][None],
            jnp.concatenate([cache_swa_v[l], kvs[:, :, 1]], axis=1)[:, -WINDOW:][None])
```

```python
import functools

import jax
import jax.numpy as jnp
from jax import lax
from jax.experimental import pallas as pl
from jax.experimental.pallas import tpu as pltpu

F32 = jnp.float32
BF16 = jnp.bfloat16
U32 = jnp.uint32

CHUNK = 64
GLA_HEADS = 4
GLA_DK = 128
GLA_DV = 256
GLA_GATE_RANK = 16
GLA_GATE_NORM = 16.0
SWA_Q_HEADS = 16
SWA_KV_HEADS = 4
SWA_GROUP = SWA_Q_HEADS // SWA_KV_HEADS
HEAD_DIM = 64
WINDOW = 128
WINDOW_CHUNKS = WINDOW // CHUNK
ALIBI_MAX_BIAS = 8.0
N_EXPERTS = 32
TOP_K = 4
SWIGLU_LIMIT = 7.0
SWIGLU_ALPHA = 1.702
RMS_EPS = 1e-5
NEG_INF = -1e30

QA_W = GLA_HEADS * GLA_DK
VA_W = GLA_HEADS * GLA_DV
QB_W = SWA_Q_HEADS * HEAD_DIM
KB_W = SWA_KV_HEADS * HEAD_DIM
MIX_W = VA_W + QB_W
MAIN_W = 2 * QA_W + 2 * VA_W + QB_W + 2 * KB_W
QB_COL = (2 * QA_W + 2 * VA_W) // QB_W
KB_COL = (2 * QA_W + 2 * VA_W + QB_W) // KB_W
LANE = 128
LR_PAD = LANE

PROJ_TN = 1536
ROW_TILE = 512
MOE_BM = 512
MOE_TF = 1024
CAST_MAX_ROWS = 256
OUT_TM = 256
COMBINE_TM = 256
ROUTE_TILE = 512
VMEM_LIMIT = 48 * 1024 * 1024
MOE_VMEM_LIMIT = 58 * 1024 * 1024


def _cparams(sem, vmem=VMEM_LIMIT):
    return pltpu.CompilerParams(dimension_semantics=sem, vmem_limit_bytes=vmem)


def _dot(a, b):
    return jnp.dot(a, b, preferred_element_type=F32)


def _dot_tn(a, b):
    return lax.dot_general(a, b, (((0,), (0,)), ((), ())), preferred_element_type=F32)


def _dot_nt(a, b):
    return lax.dot_general(a, b, (((1,), (1,)), ((), ())), preferred_element_type=F32)


def _rms(x):
    return x * lax.rsqrt(jnp.mean(x * x, axis=-1, keepdims=True) + RMS_EPS)


def _hi_lo(x):
    hi = x.astype(BF16)
    return hi, (x - hi.astype(F32)).astype(BF16)


def _pack_halves(x):
    c = x.shape[1] // 2
    hi = lax.bitcast_convert_type(x[:, :c].astype(BF16).astype(F32), U32)
    lo = lax.bitcast_convert_type(x[:, c:].astype(BF16).astype(F32), U32)
    return hi | (lo >> 16)


def _unpack_halves(w):
    hi = lax.bitcast_convert_type(w & jnp.uint32(0xFFFF0000), F32)
    lo = lax.bitcast_convert_type(w << 16, F32)
    return hi, lo


def _in_proj_kernel(x_ref, nw_ref, w_ref, wlr_ref, out_ref, lr_ref, kv_ref, xn_ref, *, n_col_steps):
    j = pl.program_id(1)

    @pl.when(j == 0)
    def _():
        xnb = (_rms(x_ref[...]) * nw_ref[...]).astype(BF16)
        xn_ref[...] = xnb
        lr_ref[...] = _dot(xnb, wlr_ref[...]).astype(BF16)

    acc = _dot(xn_ref[...], w_ref[...])
    out_ref[...] = acc.astype(BF16)

    @pl.when(j == n_col_steps - 1)
    def _():
        kv_ref[...] = acc[:, PROJ_TN - 2 * KB_W:]


def _in_proj(x2d, norm_w, w_main, w_lr):
    n, d = x2d.shape
    tm = min(ROW_TILE, n)
    n_col_steps = MAIN_W // PROJ_TN
    return pl.pallas_call(
        functools.partial(_in_proj_kernel, n_col_steps=n_col_steps),
        out_shape=(jax.ShapeDtypeStruct((n, MAIN_W), BF16),
                   jax.ShapeDtypeStruct((n, LR_PAD), BF16),
                   jax.ShapeDtypeStruct((n, 2 * KB_W), F32)),
        grid=(n // tm, n_col_steps),
        in_specs=[pl.BlockSpec((tm, d), lambda i, j: (i, 0)),
                  pl.BlockSpec((1, d), lambda i, j: (0, 0)),
                  pl.BlockSpec((d, PROJ_TN), lambda i, j: (0, j)),
                  pl.BlockSpec((d, LR_PAD), lambda i, j: (0, 0))],
        out_specs=(pl.BlockSpec((tm, PROJ_TN), lambda i, j: (i, j)),
                   pl.BlockSpec((tm, LR_PAD), lambda i, j: (i, 0)),
                   pl.BlockSpec((tm, 2 * KB_W), lambda i, j: (i, 0))),
        scratch_shapes=[pltpu.VMEM((tm, d), BF16)],
        compiler_params=_cparams(("parallel", "arbitrary")),
        name="in_proj",
    )(x2d, norm_w, w_main, w_lr)


def _gla_chunk(rows, chunk, q_ref, k_ref, v_ref, ga_ref, lr_ref, gup, gb, nw, tri, s_scr, o_ref):
    z = _dot(lr_ref[rows, :], gup) + gb
    fc = (jnp.minimum(z, 0.0) - jnp.log1p(jnp.exp(-jnp.abs(z)))) * (1.0 / GLA_GATE_NORM)
    hi, lo = _hi_lo(fc)
    g = _dot(tri, hi) + _dot(tri, lo)
    g_end = g[chunk - 1:chunk, :]
    k_dec = (k_ref[rows, :].astype(F32) * jnp.exp(g_end - g)).astype(BF16)
    a_end = jnp.exp(g_end)
    for h in range(GLA_HEADS):
        ks = slice(h * GLA_DK, (h + 1) * GLA_DK)
        vs = slice(h * GLA_DV, (h + 1) * GLA_DV)
        upd = _dot_tn(v_ref[rows, vs], k_dec[:, ks])
        s_new = s_scr[h] * a_end[:, ks] + upd
        s_scr[h] = s_new
        o = _dot_nt(q_ref[rows, ks], s_new.astype(BF16)) * (GLA_DK ** -0.5)
        gate = ga_ref[rows, vs].astype(F32)
        o_ref[rows, vs] = (_rms(o) * nw * (gate * jax.nn.sigmoid(gate))).astype(BF16)


def _alibi_slopes():
    s = jnp.exp2(-ALIBI_MAX_BIAS * jnp.arange(1, SWA_Q_HEADS + 1, dtype=F32) / SWA_Q_HEADS)
    return s.reshape(SWA_KV_HEADS, SWA_GROUP)


def _stacked_bias(q_off, lq, lk):
    dist = jnp.abs(q_off + jnp.arange(lq)[:, None] - jnp.arange(lk)[None, :]).astype(F32)
    b = -_alibi_slopes()[:, :, None, None] * dist
    return b.reshape(SWA_KV_HEADS, SWA_GROUP * lq, lk)


def _stacked_sinks(sinks, lq):
    s = sinks.astype(F32).reshape(SWA_KV_HEADS, SWA_GROUP, 1)
    return jnp.broadcast_to(s, (SWA_KV_HEADS, SWA_GROUP, lq)).reshape(SWA_KV_HEADS, SWA_GROUP * lq, 1)


def _attend_group(q, keys, vals, bias, sink, valid):
    lq = q.shape[0]
    qs = jnp.concatenate([q[:, j * HEAD_DIM:(j + 1) * HEAD_DIM] for j in range(SWA_GROUP)], axis=0)
    s = _dot_nt(qs, keys) * (HEAD_DIM ** -0.5) + bias
    if valid is not None:
        s = jnp.where(valid, s, NEG_INF)
    m = jnp.maximum(jnp.max(s, axis=-1, keepdims=True), sink)
    p = jnp.exp(s - m)
    denom = jnp.sum(p, axis=-1, keepdims=True) + jnp.exp(sink - m)
    o = _dot(p.astype(BF16), vals) / denom
    return jnp.concatenate([o[j * lq:(j + 1) * lq, :] for j in range(SWA_GROUP)], axis=1)


def _mix_prompt_kernel(*refs, n_chunks, n_steps, n_cast, cast_rows):
    (q_ref, k_ref, v_ref, ga_ref, lr_ref, qb_ref, kc_ref, vc_ref, kp_ref, vp_ref,
     gup_ref, gb_ref, nw_ref, tri_ref, bias_ref, sink_ref) = refs[:16]
    w_hbm = refs[16:16 + n_cast]
    o_ref, st_ref = refs[16 + n_cast:18 + n_cast]
    wb_hbm = refs[18 + n_cast:18 + 2 * n_cast]
    s_scr, k_scr, v_scr = refs[18 + 2 * n_cast:21 + 2 * n_cast]
    if n_cast:
        win, wout, isem, osem = refs[21 + 2 * n_cast:]
    t = pl.program_id(1)
    back = WINDOW_CHUNKS * CHUNK
    span = back + CHUNK
    pairs = n_chunks // 2
    step_id = pl.program_id(0) * pl.num_programs(1) + t
    total = n_steps * pairs

    def w_in(j, piece, slot):
        return pltpu.make_async_copy(w_hbm[j].at[pl.ds(piece * cast_rows, cast_rows)],
                                     win.at[slot, j], isem.at[slot])

    def w_out(j, piece, slot):
        return pltpu.make_async_copy(wout.at[slot, j],
                                     wb_hbm[j].at[pl.ds(piece * cast_rows, cast_rows)], osem.at[slot])

    @pl.when(t == 0)
    def _():
        s_scr[...] = jnp.zeros_like(s_scr)

    k_scr[0:back, :] = kp_ref[...]
    v_scr[0:back, :] = vp_ref[...]
    k_scr[back:, :] = kc_ref[...]
    v_scr[back:, :] = vc_ref[...]
    key_idx = lax.broadcasted_iota(jnp.int32, (SWA_GROUP * CHUNK, span), 1)
    gup, gb, nw, tri = gup_ref[...], gb_ref[...], nw_ref[...], tri_ref[...]

    def chunk_step(c):
        r0 = pl.multiple_of(c * CHUNK, CHUNK)
        rows = pl.ds(r0, CHUNK)
        _gla_chunk(rows, CHUNK, q_ref, k_ref, v_ref, ga_ref, lr_ref, gup, gb, nw, tri, s_scr, o_ref)
        valid = jnp.logical_or(t > 0, key_idx >= back - c * CHUNK)
        for g in range(SWA_KV_HEADS):
            hs = slice(g * HEAD_DIM, (g + 1) * HEAD_DIM)
            qs = slice(g * SWA_GROUP * HEAD_DIM, (g + 1) * SWA_GROUP * HEAD_DIM)
            o = _attend_group(qb_ref[rows, qs], k_scr[pl.ds(r0, span), hs],
                              v_scr[pl.ds(r0, span), hs], bias_ref[g], sink_ref[g], valid)
            o_ref[rows, VA_W + g * SWA_GROUP * HEAD_DIM:VA_W + (g + 1) * SWA_GROUP * HEAD_DIM] = o.astype(BF16)

    def pair_step(p, carry):
        piece = step_id * pairs + p
        slot = piece % 2
        if n_cast:
            @pl.when(piece == 0)
            def _():
                for j in range(n_cast):
                    w_in(j, 0, 0).start()

            for j in range(n_cast):
                w_in(j, piece, slot).wait()

            @pl.when(piece + 1 < total)
            def _():
                for j in range(n_cast):
                    w_in(j, piece + 1, 1 - slot).start()

            @pl.when(piece >= 2)
            def _():
                for j in range(n_cast):
                    w_out(j, piece - 2, slot).wait()

        chunk_step(2 * p)
        chunk_step(2 * p + 1)
        if n_cast:
            for j in range(n_cast):
                wout[slot, j] = win[slot, j].astype(BF16)
            for j in range(n_cast):
                w_out(j, piece, slot).start()
        return carry

    lax.fori_loop(0, pairs, pair_step, 0)

    @pl.when(t == pl.num_programs(1) - 1)
    def _():
        st_ref[0] = s_scr[...]

    if n_cast:
        @pl.when(step_id == n_steps - 1)
        def _():
            for piece in (total - 2, total - 1):
                for j in range(n_cast):
                    w_out(j, piece, piece % 2).wait()


def _gla_consts(gup, gate_b, norm_w, chunk):
    tri = (jnp.arange(chunk)[:, None] >= jnp.arange(chunk)[None, :]).astype(BF16)
    specs = [pl.BlockSpec((LR_PAD, QA_W), lambda *_: (0, 0)),
             pl.BlockSpec((1, QA_W), lambda *_: (0, 0)),
             pl.BlockSpec((1, GLA_DV), lambda *_: (0, 0)),
             pl.BlockSpec((chunk, chunk), lambda *_: (0, 0))]
    return specs, [gup, gate_b, norm_w, tri]


def _mix_prompt(proj, lr, gup, gate_b, norm_w, sinks, cast_ws=(), *, batch, seq):
    rows = min(ROW_TILE, seq)
    n_chunks = rows // CHUNK
    nt = seq // rows
    back = WINDOW_CHUNKS * CHUNK
    pb = rows // back
    bias = _stacked_bias(back, CHUNK, back + CHUNK)
    sink = _stacked_sinks(sinks, CHUNK)
    rb = lambda b, t: b * nt + t
    prev = lambda b, t: jnp.maximum(rb(b, t) * pb - 1, 0)
    cspecs, cargs = _gla_consts(gup, gate_b, norm_w, CHUNK)
    n_cast = len(cast_ws)
    cast_rows, cast_scratch = 0, []
    if n_cast:
        assert n_chunks % 2 == 0
        w_rows, w_cols = cast_ws[0].shape
        pieces = batch * nt * (n_chunks // 2)
        assert all(w.shape == (w_rows, w_cols) for w in cast_ws) and w_rows % pieces == 0 and pieces >= 2
        cast_rows = w_rows // pieces
        cast_scratch = [pltpu.VMEM((2, n_cast, cast_rows, w_cols), F32),
                        pltpu.VMEM((2, n_cast, cast_rows, w_cols), BF16),
                        pltpu.SemaphoreType.DMA((2,)), pltpu.SemaphoreType.DMA((2,))]
    any_spec = pl.BlockSpec(memory_space=pl.ANY)
    return pl.pallas_call(
        functools.partial(_mix_prompt_kernel, n_chunks=n_chunks, n_steps=batch * nt, n_cast=n_cast,
                          cast_rows=cast_rows),
        out_shape=(jax.ShapeDtypeStruct((batch * seq, MIX_W), BF16),
                   jax.ShapeDtypeStruct((batch, GLA_HEADS, GLA_DV, GLA_DK), F32),
                   *[jax.ShapeDtypeStruct(w.shape, BF16) for w in cast_ws]),
        grid=(batch, nt),
        in_specs=[pl.BlockSpec((rows, QA_W), lambda b, t: (rb(b, t), 0)),
                  pl.BlockSpec((rows, QA_W), lambda b, t: (rb(b, t), 1)),
                  pl.BlockSpec((rows, VA_W), lambda b, t: (rb(b, t), 1)),
                  pl.BlockSpec((rows, VA_W), lambda b, t: (rb(b, t), 2)),
                  pl.BlockSpec((rows, LR_PAD), lambda b, t: (rb(b, t), 0)),
                  pl.BlockSpec((rows, QB_W), lambda b, t: (rb(b, t), QB_COL)),
                  pl.BlockSpec((rows, KB_W), lambda b, t: (rb(b, t), KB_COL)),
                  pl.BlockSpec((rows, KB_W), lambda b, t: (rb(b, t), KB_COL + 1)),
                  pl.BlockSpec((back, KB_W), lambda b, t: (prev(b, t), KB_COL)),
                  pl.BlockSpec((back, KB_W), lambda b, t: (prev(b, t), KB_COL + 1)),
                  *cspecs,
                  pl.BlockSpec(bias.shape, lambda b, t: (0, 0, 0)),
                  pl.BlockSpec(sink.shape, lambda b, t: (0, 0, 0)),
                  *[any_spec] * n_cast],
        out_specs=(pl.BlockSpec((rows, MIX_W), lambda b, t: (rb(b, t), 0)),
                   pl.BlockSpec((1, GLA_HEADS, GLA_DV, GLA_DK), lambda b, t: (b, 0, 0, 0)),
                   *[any_spec] * n_cast),
        scratch_shapes=[pltpu.VMEM((GLA_HEADS, GLA_DV, GLA_DK), F32),
                        pltpu.VMEM((rows + back, KB_W), BF16),
                        pltpu.VMEM((rows + back, KB_W), BF16),
                        *cast_scratch],
        compiler_params=_cparams(("arbitrary", "arbitrary")),
        name="mix_prompt",
    )(proj, proj, proj, proj, lr, proj, proj, proj, proj, proj, *cargs, bias, sink, *cast_ws)


def _mix_sample_kernel(q_ref, k_ref, v_ref, ga_ref, lr_ref, qb_ref, kn_ref, vn_ref, wk_ref, wv_ref,
                       s0_ref, gup_ref, gb_ref, nw_ref, tri_ref, bias_ref, sink_ref,
                       o_ref, st_ref, s_scr, *, seq):
    s_scr[...] = s0_ref[0]
    _gla_chunk(slice(0, seq), seq, q_ref, k_ref, v_ref, ga_ref, lr_ref,
               gup_ref[...], gb_ref[...], nw_ref[...], tri_ref[...], s_scr, o_ref)
    st_ref[0] = s_scr[...]
    keys = jnp.concatenate([wk_ref[0].astype(BF16), kn_ref[...]], axis=0)
    vals = jnp.concatenate([wv_ref[0].astype(BF16), vn_ref[...]], axis=0)
    for g in range(SWA_KV_HEADS):
        hs = slice(g * HEAD_DIM, (g + 1) * HEAD_DIM)
        qs = slice(g * SWA_GROUP * HEAD_DIM, (g + 1) * SWA_GROUP * HEAD_DIM)
        o = _attend_group(qb_ref[:, qs], keys[:, hs], vals[:, hs], bias_ref[g], sink_ref[g], None)
        o_ref[:, VA_W + g * SWA_GROUP * HEAD_DIM:VA_W + (g + 1) * SWA_GROUP * HEAD_DIM] = o.astype(BF16)


def _mix_sample(proj, lr, gup, gate_b, norm_w, sinks, s0_t, win_k, win_v, *, batch, seq):
    bias = _stacked_bias(WINDOW, seq, WINDOW + seq)
    sink = _stacked_sinks(sinks, seq)
    cspecs, cargs = _gla_consts(gup, gate_b, norm_w, seq)
    state_spec = pl.BlockSpec((1, GLA_HEADS, GLA_DV, GLA_DK), lambda b: (b, 0, 0, 0))
    return pl.pallas_call(
        functools.partial(_mix_sample_kernel, seq=seq),
        out_shape=(jax.ShapeDtypeStruct((batch * seq, MIX_W), BF16),
                   jax.ShapeDtypeStruct((batch, GLA_HEADS, GLA_DV, GLA_DK), F32)),
        grid=(batch,),
        in_specs=[pl.BlockSpec((seq, QA_W), lambda b: (b, 0)),
                  pl.BlockSpec((seq, QA_W), lambda b: (b, 1)),
                  pl.BlockSpec((seq, VA_W), lambda b: (b, 1)),
                  pl.BlockSpec((seq, VA_W), lambda b: (b, 2)),
                  pl.BlockSpec((seq, LR_PAD), lambda b: (b, 0)),
                  pl.BlockSpec((seq, QB_W), lambda b: (b, QB_COL)),
                  pl.BlockSpec((seq, KB_W), lambda b: (b, KB_COL)),
                  pl.BlockSpec((seq, KB_W), lambda b: (b, KB_COL + 1)),
                  pl.BlockSpec((1, WINDOW, KB_W), lambda b: (b, 0, 0)),
                  pl.BlockSpec((1, WINDOW, KB_W), lambda b: (b, 0, 0)),
                  state_spec,
                  *cspecs,
                  pl.BlockSpec(bias.shape, lambda b: (0, 0, 0)),
                  pl.BlockSpec(sink.shape, lambda b: (0, 0, 0))],
        out_specs=(pl.BlockSpec((seq, MIX_W), lambda b: (b, 0)), state_spec),
        scratch_shapes=[pltpu.VMEM((GLA_HEADS, GLA_DV, GLA_DK), F32)],
        compiler_params=_cparams(("parallel",)),
        name="mix_sample",
    )(proj, proj, proj, proj, lr, proj, proj, proj, win_k, win_v, s0_t, *cargs, bias, sink)


def _out_proj_kernel(mp_ref, xp_ref, ms_ref, xs_ref, w_ref, nw_ref, rwh_ref, rwl_ref, rb_ref,
                     hp_ref, hs_ref, xn_ref, lg_ref, *, p_tiles):
    i = pl.program_id(0)

    def tile(m_ref, x_ref, h_ref):
        h = x_ref[...] + _dot(m_ref[...], w_ref[...])
        h_ref[...] = h
        xn = _rms(h) * nw_ref[...]
        xn_ref[...] = _pack_halves(xn)
        xh, xl = _hi_lo(xn)
        rwh = rwh_ref[...]
        lg_ref[...] = _dot(xh, rwh) + _dot(xl, rwh) + _dot(xh, rwl_ref[...]) + rb_ref[...]

    @pl.when(i < p_tiles)
    def _():
        tile(mp_ref, xp_ref, hp_ref)

    @pl.when(i >= p_tiles)
    def _():
        tile(ms_ref, xs_ref, hs_ref)


def _out_proj(mix_p, x_p, mix_s, x_s, w_out, norm_w, rw_hi, rw_lo, router_b):
    n_p, d = x_p.shape
    n_s = x_s.shape[0]
    tm = min(OUT_TM, n_p, n_s)
    p_tiles, s_tiles = n_p // tm, n_s // tm
    n_all = n_p + n_s
    pidx = lambda i: (jnp.minimum(i, p_tiles - 1), 0)
    sidx = lambda i: (jnp.maximum(i - p_tiles, 0), 0)
    const = lambda i: (0, 0)
    return pl.pallas_call(
        functools.partial(_out_proj_kernel, p_tiles=p_tiles),
        out_shape=(jax.ShapeDtypeStruct((n_p, d), F32),
                   jax.ShapeDtypeStruct((n_s, d), F32),
                   jax.ShapeDtypeStruct((n_all, d // 2), U32),
                   jax.ShapeDtypeStruct((n_all, N_EXPERTS), F32)),
        grid=(p_tiles + s_tiles,),
        in_specs=[pl.BlockSpec((tm, MIX_W), pidx),
                  pl.BlockSpec((tm, d), pidx),
                  pl.BlockSpec((tm, MIX_W), sidx),
                  pl.BlockSpec((tm, d), sidx),
                  pl.BlockSpec((MIX_W, d), const),
                  pl.BlockSpec((1, d), const),
                  pl.BlockSpec((d, N_EXPERTS), const),
                  pl.BlockSpec((d, N_EXPERTS), const),
                  pl.BlockSpec((1, N_EXPERTS), const)],
        out_specs=(pl.BlockSpec((tm, d), pidx),
                   pl.BlockSpec((tm, d), sidx),
                   pl.BlockSpec((tm, d // 2), lambda i: (i, 0)),
                   pl.BlockSpec((tm, N_EXPERTS), lambda i: (i, 0))),
        compiler_params=_cparams(("arbitrary",)),
        name="out_proj",
    )(mix_p, x_p, mix_s, x_s, w_out, norm_w, rw_hi, rw_lo, router_b)


def _route(logits, bm):
    n = logits.shape[0]
    tk = n * TOP_K
    top_val, top_idx = lax.top_k(logits, TOP_K)
    weights = jax.nn.softmax(top_val, axis=-1)
    flat_e = top_idx.reshape(tk).astype(jnp.int32)
    assert tk % ROUTE_TILE == 0
    onehot = (flat_e[:, None] == jnp.arange(N_EXPERTS, dtype=jnp.int32)[None, :])
    onehot = onehot.reshape(tk // ROUTE_TILE, ROUTE_TILE, N_EXPERTS)
    lower = (jnp.arange(ROUTE_TILE)[:, None] > jnp.arange(ROUTE_TILE)[None, :]).astype(BF16)
    within = jnp.einsum('ij,tje->tie', lower, onehot.astype(BF16), preferred_element_type=F32)
    totals = jnp.sum(onehot.astype(F32), axis=1)
    before = jnp.cumsum(totals, axis=0) - totals
    counts = (before[-1] + totals[-1]).astype(jnp.int32)
    padded = (counts + bm - 1) // bm * bm
    pad_end = jnp.cumsum(padded)
    pad_start = pad_end - padded
    row_of = within + before[:, None, :] + pad_start.astype(F32)[None, None, :]
    dest = jnp.sum(jnp.where(onehot, row_of, 0.0), axis=-1).astype(jnp.int32).reshape(tk)
    n_blocks = -(-(tk + N_EXPERTS * (bm - 1)) // bm)
    n_rows = n_blocks * bm
    block_e = jnp.minimum(jnp.searchsorted(pad_end, jnp.arange(n_blocks) * bm, side='right'),
                          N_EXPERTS - 1).astype(jnp.int32)
    asg = jnp.arange(tk, dtype=jnp.int32)
    row_asg = jnp.full((n_rows,), -1, jnp.int32).at[dest].set(asg, unique_indices=True,
                                                              mode='promise_in_bounds')
    is_pad = row_asg < 0
    row_tok = jnp.where(is_pad, 0, row_asg // TOP_K)
    row_slot = jnp.where(is_pad, tk + jnp.arange(n_rows, dtype=jnp.int32) % bm,
                         (row_asg % TOP_K) * n + row_asg // TOP_K)
    n_real = (pad_end[-1] // bm).astype(jnp.int32).reshape(1)
    return weights, row_tok, row_slot, block_e, n_real


def _row_copy(src_ref, dst_ref, sem, src_row, dst_row, n_rows=1):
    return pltpu.make_async_copy(src_ref.at[pl.ds(src_row, n_rows)],
                                 dst_ref.at[pl.ds(dst_row, n_rows)], sem)


def _moe_kernel(be_ref, nreal_ref, idx_ref, idx_next_ref, slot_prev_ref, x_hbm,
                gw_ref, gb_ref, uw_ref, ub_ref, dw_ref, db_ref, ys_hbm,
                acc_ref, xb_ref, xg_ref, yb_ref, gsem, ssem, *, bm, nf):
    blk = pl.program_id(0)
    f = pl.program_id(1)
    n_real = nreal_ref[0]
    cur = blk % 2

    def gather(rows_ref, i):
        return _row_copy(x_hbm, xg_ref, gsem, rows_ref[0, 0, i], i)

    def scatter(i):
        return _row_copy(yb_ref, ys_hbm, ssem, i, slot_prev_ref[0, 0, i])

    def wait_gather():
        _row_copy(x_hbm, xg_ref, gsem, 0, 0, bm).wait()

    def wait_scatter():
        _row_copy(yb_ref, ys_hbm, ssem, 0, 0, bm).wait()

    def start_all(make):
        def body(i, carry):
            make(i).start()
            return carry
        lax.fori_loop(0, bm, body, 0)

    def unpack_rows(slot):
        hi, lo = _unpack_halves(xg_ref[...])
        xb_ref[slot] = jnp.concatenate([hi.astype(BF16), lo.astype(BF16)], axis=1)

    def step(fs):
        if fs == 0:
            @pl.when(blk == 0)
            def _():
                yb_ref[...] = jnp.zeros_like(yb_ref)
                start_all(lambda i: gather(idx_ref, i))
                wait_gather()
                unpack_rows(0)

            for i in range(bm):
                gather(idx_next_ref, i).start()
        if fs == nf - 1:
            wait_gather()
            for i in range(bm):
                scatter(i).start()
            unpack_rows(1 - cur)

        xb = xb_ref[cur]
        g = jnp.minimum(_dot(xb, gw_ref[0]) + gb_ref[0], SWIGLU_LIMIT)
        u = jnp.clip(_dot(xb, uw_ref[0]) + ub_ref[0], -SWIGLU_LIMIT, SWIGLU_LIMIT)
        hmid = g * jax.nn.sigmoid(SWIGLU_ALPHA * g) * (u + 1.0)
        part = _dot(hmid.astype(BF16), dw_ref[0])
        if fs > 0:
            part = acc_ref[...] + part
        if fs < nf - 1:
            acc_ref[...] = part
        else:
            wait_scatter()
            yb_ref[...] = _pack_halves(part + db_ref[0])

    for fs in range(nf):
        pl.when(jnp.logical_and(blk < n_real, f == fs))(functools.partial(step, fs))

    @pl.when(jnp.logical_and(blk == n_real, f == 0))
    def _():
        start_all(scatter)
        wait_scatter()


def _moe_experts(x_all, row_tok, row_slot, block_e, n_real, gate_w, gate_b, up_w, up_b, down_w, down_b,
                 *, bm):
    n_rows = row_tok.shape[0]
    n, dw = x_all.shape
    d = 2 * dw
    n_blocks = n_rows // bm
    d_ff = gate_w.shape[-1]
    tf = min(MOE_TF, d_ff)
    nf = d_ff // tf
    assert nf >= 2, "the gather started in the first d_ff step is drained in the last one"
    idx = row_tok.reshape(n_blocks, 1, bm)
    trash = TOP_K * n + jnp.arange(bm, dtype=jnp.int32)
    slot_prev = jnp.concatenate([trash, row_slot]).reshape(n_blocks + 1, 1, bm)

    def e_of(b, be, nr):
        return be[jnp.minimum(b, nr[0] - 1)]

    def f_of(b, f, nr):
        return jnp.where(b < nr[0], f, nf - 1)

    smem = functools.partial(pl.BlockSpec, (1, 1, bm), memory_space=pltpu.SMEM)
    return pl.pallas_call(
        functools.partial(_moe_kernel, bm=bm, nf=nf),
        out_shape=jax.ShapeDtypeStruct((TOP_K * n + bm, dw), U32),
        grid_spec=pltpu.PrefetchScalarGridSpec(
            num_scalar_prefetch=2,
            grid=(n_blocks, nf),
            in_specs=[
                smem(lambda b, f, be, nr: (b, 0, 0)),
                smem(lambda b, f, be, nr: (jnp.minimum(b + 1, n_blocks - 1), 0, 0)),
                smem(lambda b, f, be, nr: (b, 0, 0)),
                pl.BlockSpec(memory_space=pl.ANY),
                pl.BlockSpec((1, d, tf), lambda b, f, be, nr: (e_of(b, be, nr), 0, f_of(b, f, nr))),
                pl.BlockSpec((1, 1, tf), lambda b, f, be, nr: (e_of(b, be, nr), 0, f_of(b, f, nr))),
                pl.BlockSpec((1, d, tf), lambda b, f, be, nr: (e_of(b, be, nr), 0, f_of(b, f, nr))),
                pl.BlockSpec((1, 1, tf), lambda b, f, be, nr: (e_of(b, be, nr), 0, f_of(b, f, nr))),
                pl.BlockSpec((1, tf, d), lambda b, f, be, nr: (e_of(b, be, nr), f_of(b, f, nr), 0)),
                pl.BlockSpec((1, 1, d), lambda b, f, be, nr: (e_of(b, be, nr), 0, 0)),
            ],
            out_specs=pl.BlockSpec(memory_space=pl.ANY),
            scratch_shapes=[pltpu.VMEM((bm, d), F32), pltpu.VMEM((2, bm, d), BF16),
                            pltpu.VMEM((bm, dw), U32), pltpu.VMEM((bm, dw), U32),
                            pltpu.SemaphoreType.DMA, pltpu.SemaphoreType.DMA]),
        compiler_params=_cparams(("arbitrary", "arbitrary"), MOE_VMEM_LIMIT),
        name="moe_experts",
    )(block_e, n_real, idx, idx, slot_prev, x_all, gate_w, gate_b, up_w, up_b, down_w, down_b)


def _combine_kernel(*refs):
    y_refs, (w_ref, h_ref, nw_ref, out_ref) = refs[:TOP_K], refs[TOP_K:]
    w = w_ref[...]
    c = out_ref.shape[1] // 2
    left, right = h_ref[:, :c], h_ref[:, c:]
    for k in range(TOP_K):
        hi, lo = _unpack_halves(y_refs[k][...])
        left = left + w[:, k:k + 1] * hi
        right = right + w[:, k:k + 1] * lo
    ms = jnp.sum(left * left, axis=-1, keepdims=True) + jnp.sum(right * right, axis=-1, keepdims=True)
    scale = lax.rsqrt(ms / (2 * c) + RMS_EPS)
    out_ref[:, :c] = left * scale * nw_ref[:, :c]
    out_ref[:, c:] = right * scale * nw_ref[:, c:]


def _combine(y_slots, weights, h, final_norm_w, *, tok_off):
    n, d = h.shape
    n_all = weights.shape[0]
    tm = min(COMBINE_TM, n)
    ob = tok_off // tm
    slot_blocks = n_all // tm
    y_specs = [pl.BlockSpec((tm, d // 2), functools.partial(lambda k, i: (k * slot_blocks + ob + i, 0), k))
               for k in range(TOP_K)]
    return pl.pallas_call(
        _combine_kernel,
        out_shape=jax.ShapeDtypeStruct((n, d), F32),
        grid=(n // tm,),
        in_specs=[*y_specs,
                  pl.BlockSpec((tm, TOP_K), lambda i: (i + ob, 0)),
                  pl.BlockSpec((tm, d), lambda i: (i, 0)),
                  pl.BlockSpec((1, d), lambda i: (0, 0))],
        out_specs=pl.BlockSpec((tm, d), lambda i: (i, 0)),
        compiler_params=_cparams(("parallel",)),
        name="moe_combine",
    )(*([y_slots] * TOP_K), weights, h, final_norm_w)


def _split_w_in(w_in):
    lr0 = 2 * QA_W + VA_W
    w_main = jnp.concatenate([w_in[:, :lr0], w_in[:, lr0 + GLA_GATE_RANK:]], axis=1).astype(BF16)
    w_lr = jnp.pad(w_in[:, lr0:lr0 + GLA_GATE_RANK], ((0, 0), (0, LR_PAD - GLA_GATE_RANK))).astype(BF16)
    return w_main, w_lr


def kernel(x_prompt, x_sample, state_gla, cache_swa_k, cache_swa_v, attn_norm_w, w_in, gla_gate_up_w, gla_gate_b, gla_out_norm_w, swa_sinks, w_out, ffn_norm_w, router_w, router_b, expert_gate_w, expert_gate_b, expert_up_w, expert_up_b, expert_down_w, expert_down_b, final_norm_w):
    depth = w_in.shape[0]
    assert depth == 1, "the combine applies the final norm, so exactly one layer is supported"
    bp, tp, d = x_prompt.shape
    bs, ts, _ = x_sample.shape
    n_p, n_s = bp * tp, bs * ts
    hp = x_prompt.reshape(n_p, d)
    hs = x_sample.reshape(n_s, d)
    l = 0
    w_main, w_lr = _split_w_in(w_in[l])
    gup = jnp.pad(gla_gate_up_w[l], ((0, LR_PAD - GLA_GATE_RANK), (0, 0))).astype(BF16)
    gate_b = gla_gate_b[l].reshape(1, QA_W)
    onw = gla_out_norm_w[l].reshape(1, GLA_DV)
    anw = attn_norm_w[l].reshape(1, d)
    rw_hi = router_w[l].astype(BF16)
    rw_lo = (router_w[l] - rw_hi.astype(F32)).astype(BF16)

    proj_p, lr_p, kv_p = _in_proj(hp, anw, w_main, w_lr)
    expert_ws = (expert_gate_w[l], expert_up_w[l], expert_down_w[l])
    flat_ws = tuple(w.reshape(-1, w.shape[-1]) for w in expert_ws)
    pieces = n_p // (2 * CHUNK)
    on_side = (len({w.shape for w in flat_ws}) == 1 and pieces >= 2 and tp % (2 * CHUNK) == 0
               and flat_ws[0].shape[0] % pieces == 0 and flat_ws[0].shape[0] // pieces <= CAST_MAX_ROWS)
    mix_p, st_p, *flat_wb = _mix_prompt(proj_p, lr_p, gup, gate_b, onw, swa_sinks[l],
                                        flat_ws if on_side else (), batch=bp, seq=tp)
    if not on_side:
        flat_wb = [w.astype(BF16) for w in flat_ws]
    gate_w16, up_w16, down_w16 = (wb.reshape(w.shape) for wb, w in zip(flat_wb, expert_ws))
    proj_s, lr_s, kv_s = _in_proj(hs, anw, w_main, w_lr)
    mix_s, st_s = _mix_sample(proj_s, lr_s, gup, gate_b, onw, swa_sinks[l],
                              jnp.swapaxes(state_gla[l], -1, -2),
                              cache_swa_k[l].reshape(bs, WINDOW, KB_W),
                              cache_swa_v[l].reshape(bs, WINDOW, KB_W), batch=bs, seq=ts)

    hp, hs, xn_all, lg_all = _out_proj(mix_p, hp, mix_s, hs, w_out[l].astype(BF16),
                                       ffn_norm_w[l].reshape(1, d), rw_hi, rw_lo,
                                       router_b[l].reshape(1, N_EXPERTS))

    weights, row_tok, row_slot, block_e, n_real = _route(lg_all, MOE_BM)
    y_slots = _moe_experts(xn_all, row_tok, row_slot, block_e, n_real,
                          gate_w16, expert_gate_b[l][:, None, :], up_w16, expert_up_b[l][:, None, :],
                          down_w16, expert_down_b[l][:, None, :], bm=MOE_BM)
    fin = final_norm_w.reshape(1, d)
    y_prompt = _combine(y_slots, weights, hp, fin, tok_off=0).reshape(bp, tp, d)
    y_sample = _combine(y_slots, weights, hs, fin, tok_off=n_p).reshape(bs, ts, d)

    kvp = kv_p.reshape(bp, tp, 2, SWA_KV_HEADS, HEAD_DIM)[:, -WINDOW:]
    kvs = kv_s.reshape(bs, ts, 2, SWA_KV_HEADS, HEAD_DIM)
    return (y_prompt, y_sample,
            jnp.swapaxes(st_p, -1, -2)[None], kvp[:, :, 0][None], kvp[:, :, 1][None],
            jnp.swapaxes(st_s, -1, -2)[None],
            jnp.concatenate([cache_swa_k[l], kvs[:, :, 0]], axis=1)[:, -WINDOW:][None],
            jnp.concatenate([cache_swa_v[l], kvs[:, :, 1]], axis=1)[:, -WINDOW:][None])
```

```python
import functools

import jax
import jax.numpy as jnp
from jax import lax
from jax.experimental import pallas as pl
from jax.experimental.pallas import tpu as pltpu

F32 = jnp.float32
BF16 = jnp.bfloat16

CHUNK = 64
GLA_HEADS = 4
GLA_DK = 128
GLA_DV = 256
GLA_GATE_RANK = 16
GLA_GATE_NORM = 16.0
SWA_Q_HEADS = 16
SWA_KV_HEADS = 4
SWA_GROUP = SWA_Q_HEADS // SWA_KV_HEADS
HEAD_DIM = 64
WINDOW = 128
WINDOW_CHUNKS = WINDOW // CHUNK
ALIBI_MAX_BIAS = 8.0
N_EXPERTS = 32
TOP_K = 4
SWIGLU_LIMIT = 7.0
SWIGLU_ALPHA = 1.702
RMS_EPS = 1e-5
NEG_INF = -1e30

QA_W = GLA_HEADS * GLA_DK
VA_W = GLA_HEADS * GLA_DV
QB_W = SWA_Q_HEADS * HEAD_DIM
KB_W = SWA_KV_HEADS * HEAD_DIM
MIX_W = VA_W + QB_W
MAIN_W = 2 * QA_W + 2 * VA_W + QB_W + 2 * KB_W
QB_COL = (2 * QA_W + 2 * VA_W) // QB_W
KB_COL = (2 * QA_W + 2 * VA_W + QB_W) // KB_W
LANE = 128
LR_PAD = LANE

PROJ_TN = 1536
ROW_TILE = 512
MOE_BM = 512
MOE_TF = 1024
CAST_MAX_ROWS = 256
OUT_TM = 256
COMBINE_TM = 256
ROUTE_TILE = 512
VMEM_LIMIT = 48 * 1024 * 1024
MOE_VMEM_LIMIT = 58 * 1024 * 1024


def _cparams(sem, vmem=VMEM_LIMIT):
    return pltpu.CompilerParams(dimension_semantics=sem, vmem_limit_bytes=vmem)


def _dot(a, b):
    return jnp.dot(a, b, preferred_element_type=F32)


def _dot_tn(a, b):
    return lax.dot_general(a, b, (((0,), (0,)), ((), ())), preferred_element_type=F32)


def _dot_nt(a, b):
    return lax.dot_general(a, b, (((1,), (1,)), ((), ())), preferred_element_type=F32)


def _rms(x):
    return x * lax.rsqrt(jnp.mean(x * x, axis=-1, keepdims=True) + RMS_EPS)


def _hi_lo(x):
    hi = x.astype(BF16)
    return hi, (x - hi.astype(F32)).astype(BF16)


def _in_proj_kernel(x_ref, nw_ref, w_ref, wlr_ref, out_ref, lr_ref, kv_ref, xn_ref, *, n_col_steps):
    j = pl.program_id(1)

    @pl.when(j == 0)
    def _():
        xnb = (_rms(x_ref[...]) * nw_ref[...]).astype(BF16)
        xn_ref[...] = xnb
        lr_ref[...] = _dot(xnb, wlr_ref[...]).astype(BF16)

    acc = _dot(xn_ref[...], w_ref[...])
    out_ref[...] = acc.astype(BF16)

    @pl.when(j == n_col_steps - 1)
    def _():
        kv_ref[...] = acc[:, PROJ_TN - 2 * KB_W:]


def _in_proj(x2d, norm_w, w_main, w_lr):
    n, d = x2d.shape
    tm = min(ROW_TILE, n)
    n_col_steps = MAIN_W // PROJ_TN
    return pl.pallas_call(
        functools.partial(_in_proj_kernel, n_col_steps=n_col_steps),
        out_shape=(jax.ShapeDtypeStruct((n, MAIN_W), BF16),
                   jax.ShapeDtypeStruct((n, LR_PAD), BF16),
                   jax.ShapeDtypeStruct((n, 2 * KB_W), F32)),
        grid=(n // tm, n_col_steps),
        in_specs=[pl.BlockSpec((tm, d), lambda i, j: (i, 0)),
                  pl.BlockSpec((1, d), lambda i, j: (0, 0)),
                  pl.BlockSpec((d, PROJ_TN), lambda i, j: (0, j)),
                  pl.BlockSpec((d, LR_PAD), lambda i, j: (0, 0))],
        out_specs=(pl.BlockSpec((tm, PROJ_TN), lambda i, j: (i, j)),
                   pl.BlockSpec((tm, LR_PAD), lambda i, j: (i, 0)),
                   pl.BlockSpec((tm, 2 * KB_W), lambda i, j: (i, 0))),
        scratch_shapes=[pltpu.VMEM((tm, d), BF16)],
        compiler_params=_cparams(("parallel", "arbitrary")),
        name="in_proj",
    )(x2d, norm_w, w_main, w_lr)


def _gla_chunk(rows, chunk, q_ref, k_ref, v_ref, ga_ref, lr_ref, gup, gb, nw, tri, s_scr, o_ref):
    z = _dot(lr_ref[rows, :], gup) + gb
    fc = (jnp.minimum(z, 0.0) - jnp.log1p(jnp.exp(-jnp.abs(z)))) * (1.0 / GLA_GATE_NORM)
    hi, lo = _hi_lo(fc)
    g = _dot(tri, hi) + _dot(tri, lo)
    g_end = g[chunk - 1:chunk, :]
    k_dec = (k_ref[rows, :].astype(F32) * jnp.exp(g_end - g)).astype(BF16)
    a_end = jnp.exp(g_end)
    for h in range(GLA_HEADS):
        ks = slice(h * GLA_DK, (h + 1) * GLA_DK)
        vs = slice(h * GLA_DV, (h + 1) * GLA_DV)
        upd = _dot_tn(v_ref[rows, vs], k_dec[:, ks])
        s_new = s_scr[h] * a_end[:, ks] + upd
        s_scr[h] = s_new
        o = _dot_nt(q_ref[rows, ks], s_new.astype(BF16)) * (GLA_DK ** -0.5)
        gate = ga_ref[rows, vs].astype(F32)
        o_ref[rows, vs] = (_rms(o) * nw * (gate * jax.nn.sigmoid(gate))).astype(BF16)


def _alibi_slopes():
    s = jnp.exp2(-ALIBI_MAX_BIAS * jnp.arange(1, SWA_Q_HEADS + 1, dtype=F32) / SWA_Q_HEADS)
    return s.reshape(SWA_KV_HEADS, SWA_GROUP)


def _stacked_bias(q_off, lq, lk):
    dist = jnp.abs(q_off + jnp.arange(lq)[:, None] - jnp.arange(lk)[None, :]).astype(F32)
    b = -_alibi_slopes()[:, :, None, None] * dist
    return b.reshape(SWA_KV_HEADS, SWA_GROUP * lq, lk)


def _stacked_sinks(sinks, lq):
    s = sinks.astype(F32).reshape(SWA_KV_HEADS, SWA_GROUP, 1)
    return jnp.broadcast_to(s, (SWA_KV_HEADS, SWA_GROUP, lq)).reshape(SWA_KV_HEADS, SWA_GROUP * lq, 1)


def _attend_group(q, keys, vals, bias, sink, valid):
    lq = q.shape[0]
    qs = jnp.concatenate([q[:, j * HEAD_DIM:(j + 1) * HEAD_DIM] for j in range(SWA_GROUP)], axis=0)
    s = _dot_nt(qs, keys) * (HEAD_DIM ** -0.5) + bias
    if valid is not None:
        s = jnp.where(valid, s, NEG_INF)
    m = jnp.maximum(jnp.max(s, axis=-1, keepdims=True), sink)
    p = jnp.exp(s - m)
    denom = jnp.sum(p, axis=-1, keepdims=True) + jnp.exp(sink - m)
    o = _dot(p.astype(BF16), vals) / denom
    return jnp.concatenate([o[j * lq:(j + 1) * lq, :] for j in range(SWA_GROUP)], axis=1)


def _mix_prompt_kernel(*refs, n_chunks, n_steps, n_cast, cast_rows):
    (q_ref, k_ref, v_ref, ga_ref, lr_ref, qb_ref, kc_ref, vc_ref, kp_ref, vp_ref,
     gup_ref, gb_ref, nw_ref, tri_ref, bias_ref, sink_ref) = refs[:16]
    w_hbm = refs[16:16 + n_cast]
    o_ref, st_ref = refs[16 + n_cast:18 + n_cast]
    wb_hbm = refs[18 + n_cast:18 + 2 * n_cast]
    s_scr, k_scr, v_scr = refs[18 + 2 * n_cast:21 + 2 * n_cast]
    if n_cast:
        win, wout, isem, osem = refs[21 + 2 * n_cast:]
    t = pl.program_id(1)
    back = WINDOW_CHUNKS * CHUNK
    span = back + CHUNK
    pairs = n_chunks // 2
    step_id = pl.program_id(0) * pl.num_programs(1) + t
    total = n_steps * pairs

    def w_in(j, piece, slot):
        return pltpu.make_async_copy(w_hbm[j].at[pl.ds(piece * cast_rows, cast_rows)],
                                     win.at[slot, j], isem.at[slot])

    def w_out(j, piece, slot):
        return pltpu.make_async_copy(wout.at[slot, j],
                                     wb_hbm[j].at[pl.ds(piece * cast_rows, cast_rows)], osem.at[slot])

    @pl.when(t == 0)
    def _():
        s_scr[...] = jnp.zeros_like(s_scr)

    k_scr[0:back, :] = kp_ref[...]
    v_scr[0:back, :] = vp_ref[...]
    k_scr[back:, :] = kc_ref[...]
    v_scr[back:, :] = vc_ref[...]
    key_idx = lax.broadcasted_iota(jnp.int32, (SWA_GROUP * CHUNK, span), 1)
    gup, gb, nw, tri = gup_ref[...], gb_ref[...], nw_ref[...], tri_ref[...]

    def chunk_step(c):
        r0 = pl.multiple_of(c * CHUNK, CHUNK)
        rows = pl.ds(r0, CHUNK)
        _gla_chunk(rows, CHUNK, q_ref, k_ref, v_ref, ga_ref, lr_ref, gup, gb, nw, tri, s_scr, o_ref)
        valid = jnp.logical_or(t > 0, key_idx >= back - c * CHUNK)
        for g in range(SWA_KV_HEADS):
            hs = slice(g * HEAD_DIM, (g + 1) * HEAD_DIM)
            qs = slice(g * SWA_GROUP * HEAD_DIM, (g + 1) * SWA_GROUP * HEAD_DIM)
            o = _attend_group(qb_ref[rows, qs], k_scr[pl.ds(r0, span), hs],
                              v_scr[pl.ds(r0, span), hs], bias_ref[g], sink_ref[g], valid)
            o_ref[rows, VA_W + g * SWA_GROUP * HEAD_DIM:VA_W + (g + 1) * SWA_GROUP * HEAD_DIM] = o.astype(BF16)

    def pair_step(p, carry):
        piece = step_id * pairs + p
        slot = piece % 2
        if n_cast:
            @pl.when(piece == 0)
            def _():
                for j in range(n_cast):
                    w_in(j, 0, 0).start()

            for j in range(n_cast):
                w_in(j, piece, slot).wait()

            @pl.when(piece + 1 < total)
            def _():
                for j in range(n_cast):
                    w_in(j, piece + 1, 1 - slot).start()

            @pl.when(piece >= 2)
            def _():
                for j in range(n_cast):
                    w_out(j, piece - 2, slot).wait()

        chunk_step(2 * p)
        chunk_step(2 * p + 1)
        if n_cast:
            for j in range(n_cast):
                wout[slot, j] = win[slot, j].astype(BF16)
            for j in range(n_cast):
                w_out(j, piece, slot).start()
        return carry

    lax.fori_loop(0, pairs, pair_step, 0)

    @pl.when(t == pl.num_programs(1) - 1)
    def _():
        st_ref[0] = s_scr[...]

    if n_cast:
        @pl.when(step_id == n_steps - 1)
        def _():
            for piece in (total - 2, total - 1):
                for j in range(n_cast):
                    w_out(j, piece, piece % 2).wait()


def _gla_consts(gup, gate_b, norm_w, chunk):
    tri = (jnp.arange(chunk)[:, None] >= jnp.arange(chunk)[None, :]).astype(BF16)
    specs = [pl.BlockSpec((LR_PAD, QA_W), lambda *_: (0, 0)),
             pl.BlockSpec((1, QA_W), lambda *_: (0, 0)),
             pl.BlockSpec((1, GLA_DV), lambda *_: (0, 0)),
             pl.BlockSpec((chunk, chunk), lambda *_: (0, 0))]
    return specs, [gup, gate_b, norm_w, tri]


def _mix_prompt(proj, lr, gup, gate_b, norm_w, sinks, cast_ws=(), *, batch, seq):
    rows = min(ROW_TILE, seq)
    n_chunks = rows // CHUNK
    nt = seq // rows
    back = WINDOW_CHUNKS * CHUNK
    pb = rows // back
    bias = _stacked_bias(back, CHUNK, back + CHUNK)
    sink = _stacked_sinks(sinks, CHUNK)
    rb = lambda b, t: b * nt + t
    prev = lambda b, t: jnp.maximum(rb(b, t) * pb - 1, 0)
    cspecs, cargs = _gla_consts(gup, gate_b, norm_w, CHUNK)
    n_cast = len(cast_ws)
    cast_rows, cast_scratch = 0, []
    if n_cast:
        assert n_chunks % 2 == 0
        w_rows, w_cols = cast_ws[0].shape
        pieces = batch * nt * (n_chunks // 2)
        assert all(w.shape == (w_rows, w_cols) for w in cast_ws) and w_rows % pieces == 0 and pieces >= 2
        cast_rows = w_rows // pieces
        cast_scratch = [pltpu.VMEM((2, n_cast, cast_rows, w_cols), F32),
                        pltpu.VMEM((2, n_cast, cast_rows, w_cols), BF16),
                        pltpu.SemaphoreType.DMA((2,)), pltpu.SemaphoreType.DMA((2,))]
    any_spec = pl.BlockSpec(memory_space=pl.ANY)
    return pl.pallas_call(
        functools.partial(_mix_prompt_kernel, n_chunks=n_chunks, n_steps=batch * nt, n_cast=n_cast,
                          cast_rows=cast_rows),
        out_shape=(jax.ShapeDtypeStruct((batch * seq, MIX_W), BF16),
                   jax.ShapeDtypeStruct((batch, GLA_HEADS, GLA_DV, GLA_DK), F32),
                   *[jax.ShapeDtypeStruct(w.shape, BF16) for w in cast_ws]),
        grid=(batch, nt),
        in_specs=[pl.BlockSpec((rows, QA_W), lambda b, t: (rb(b, t), 0)),
                  pl.BlockSpec((rows, QA_W), lambda b, t: (rb(b, t), 1)),
                  pl.BlockSpec((rows, VA_W), lambda b, t: (rb(b, t), 1)),
                  pl.BlockSpec((rows, VA_W), lambda b, t: (rb(b, t), 2)),
                  pl.BlockSpec((rows, LR_PAD), lambda b, t: (rb(b, t), 0)),
                  pl.BlockSpec((rows, QB_W), lambda b, t: (rb(b, t), QB_COL)),
                  pl.BlockSpec((rows, KB_W), lambda b, t: (rb(b, t), KB_COL)),
                  pl.BlockSpec((rows, KB_W), lambda b, t: (rb(b, t), KB_COL + 1)),
                  pl.BlockSpec((back, KB_W), lambda b, t: (prev(b, t), KB_COL)),
                  pl.BlockSpec((back, KB_W), lambda b, t: (prev(b, t), KB_COL + 1)),
                  *cspecs,
                  pl.BlockSpec(bias.shape, lambda b, t: (0, 0, 0)),
                  pl.BlockSpec(sink.shape, lambda b, t: (0, 0, 0)),
                  *[any_spec] * n_cast],
        out_specs=(pl.BlockSpec((rows, MIX_W), lambda b, t: (rb(b, t), 0)),
                   pl.BlockSpec((1, GLA_HEADS, GLA_DV, GLA_DK), lambda b, t: (b, 0, 0, 0)),
                   *[any_spec] * n_cast),
        scratch_shapes=[pltpu.VMEM((GLA_HEADS, GLA_DV, GLA_DK), F32),
                        pltpu.VMEM((rows + back, KB_W), BF16),
                        pltpu.VMEM((rows + back, KB_W), BF16),
                        *cast_scratch],
        compiler_params=_cparams(("arbitrary", "arbitrary")),
        name="mix_prompt",
    )(proj, proj, proj, proj, lr, proj, proj, proj, proj, proj, *cargs, bias, sink, *cast_ws)


def _mix_sample_kernel(q_ref, k_ref, v_ref, ga_ref, lr_ref, qb_ref, kn_ref, vn_ref, wk_ref, wv_ref,
                       s0_ref, gup_ref, gb_ref, nw_ref, tri_ref, bias_ref, sink_ref,
                       o_ref, st_ref, s_scr, *, seq):
    s_scr[...] = s0_ref[0]
    _gla_chunk(slice(0, seq), seq, q_ref, k_ref, v_ref, ga_ref, lr_ref,
               gup_ref[...], gb_ref[...], nw_ref[...], tri_ref[...], s_scr, o_ref)
    st_ref[0] = s_scr[...]
    keys = jnp.concatenate([wk_ref[0].astype(BF16), kn_ref[...]], axis=0)
    vals = jnp.concatenate([wv_ref[0].astype(BF16), vn_ref[...]], axis=0)
    for g in range(SWA_KV_HEADS):
        hs = slice(g * HEAD_DIM, (g + 1) * HEAD_DIM)
        qs = slice(g * SWA_GROUP * HEAD_DIM, (g + 1) * SWA_GROUP * HEAD_DIM)
        o = _attend_group(qb_ref[:, qs], keys[:, hs], vals[:, hs], bias_ref[g], sink_ref[g], None)
        o_ref[:, VA_W + g * SWA_GROUP * HEAD_DIM:VA_W + (g + 1) * SWA_GROUP * HEAD_DIM] = o.astype(BF16)


def _mix_sample(proj, lr, gup, gate_b, norm_w, sinks, s0_t, win_k, win_v, *, batch, seq):
    bias = _stacked_bias(WINDOW, seq, WINDOW + seq)
    sink = _stacked_sinks(sinks, seq)
    cspecs, cargs = _gla_consts(gup, gate_b, norm_w, seq)
    state_spec = pl.BlockSpec((1, GLA_HEADS, GLA_DV, GLA_DK), lambda b: (b, 0, 0, 0))
    return pl.pallas_call(
        functools.partial(_mix_sample_kernel, seq=seq),
        out_shape=(jax.ShapeDtypeStruct((batch * seq, MIX_W), BF16),
                   jax.ShapeDtypeStruct((batch, GLA_HEADS, GLA_DV, GLA_DK), F32)),
        grid=(batch,),
        in_specs=[pl.BlockSpec((seq, QA_W), lambda b: (b, 0)),
                  pl.BlockSpec((seq, QA_W), lambda b: (b, 1)),
                  pl.BlockSpec((seq, VA_W), lambda b: (b, 1)),
                  pl.BlockSpec((seq, VA_W), lambda b: (b, 2)),
                  pl.BlockSpec((seq, LR_PAD), lambda b: (b, 0)),
                  pl.BlockSpec((seq, QB_W), lambda b: (b, QB_COL)),
                  pl.BlockSpec((seq, KB_W), lambda b: (b, KB_COL)),
                  pl.BlockSpec((seq, KB_W), lambda b: (b, KB_COL + 1)),
                  pl.BlockSpec((1, WINDOW, KB_W), lambda b: (b, 0, 0)),
                  pl.BlockSpec((1, WINDOW, KB_W), lambda b: (b, 0, 0)),
                  state_spec,
                  *cspecs,
                  pl.BlockSpec(bias.shape, lambda b: (0, 0, 0)),
                  pl.BlockSpec(sink.shape, lambda b: (0, 0, 0))],
        out_specs=(pl.BlockSpec((seq, MIX_W), lambda b: (b, 0)), state_spec),
        scratch_shapes=[pltpu.VMEM((GLA_HEADS, GLA_DV, GLA_DK), F32)],
        compiler_params=_cparams(("parallel",)),
        name="mix_sample",
    )(proj, proj, proj, proj, lr, proj, proj, proj, win_k, win_v, s0_t, *cargs, bias, sink)


def _out_proj_kernel(mp_ref, xp_ref, ms_ref, xs_ref, w_ref, nw_ref, rwh_ref, rwl_ref, rb_ref,
                     hp_ref, hs_ref, xn_ref, lg_ref, *, p_tiles):
    i = pl.program_id(0)

    def tile(m_ref, x_ref, h_ref):
        h = x_ref[...] + _dot(m_ref[...], w_ref[...])
        h_ref[...] = h
        xn = _rms(h) * nw_ref[...]
        xn_ref[...] = xn
        xh, xl = _hi_lo(xn)
        rwh = rwh_ref[...]
        lg_ref[...] = _dot(xh, rwh) + _dot(xl, rwh) + _dot(xh, rwl_ref[...]) + rb_ref[...]

    @pl.when(i < p_tiles)
    def _():
        tile(mp_ref, xp_ref, hp_ref)

    @pl.when(i >= p_tiles)
    def _():
        tile(ms_ref, xs_ref, hs_ref)


def _out_proj(mix_p, x_p, mix_s, x_s, w_out, norm_w, rw_hi, rw_lo, router_b):
    n_p, d = x_p.shape
    n_s = x_s.shape[0]
    tm = min(OUT_TM, n_p, n_s)
    p_tiles, s_tiles = n_p // tm, n_s // tm
    n_all = n_p + n_s
    pidx = lambda i: (jnp.minimum(i, p_tiles - 1), 0)
    sidx = lambda i: (jnp.maximum(i - p_tiles, 0), 0)
    const = lambda i: (0, 0)
    return pl.pallas_call(
        functools.partial(_out_proj_kernel, p_tiles=p_tiles),
        out_shape=(jax.ShapeDtypeStruct((n_p, d), F32),
                   jax.ShapeDtypeStruct((n_s, d), F32),
                   jax.ShapeDtypeStruct((n_all, d), F32),
                   jax.ShapeDtypeStruct((n_all, N_EXPERTS), F32)),
        grid=(p_tiles + s_tiles,),
        in_specs=[pl.BlockSpec((tm, MIX_W), pidx),
                  pl.BlockSpec((tm, d), pidx),
                  pl.BlockSpec((tm, MIX_W), sidx),
                  pl.BlockSpec((tm, d), sidx),
                  pl.BlockSpec((MIX_W, d), const),
                  pl.BlockSpec((1, d), const),
                  pl.BlockSpec((d, N_EXPERTS), const),
                  pl.BlockSpec((d, N_EXPERTS), const),
                  pl.BlockSpec((1, N_EXPERTS), const)],
        out_specs=(pl.BlockSpec((tm, d), pidx),
                   pl.BlockSpec((tm, d), sidx),
                   pl.BlockSpec((tm, d), lambda i: (i, 0)),
                   pl.BlockSpec((tm, N_EXPERTS), lambda i: (i, 0))),
        compiler_params=_cparams(("arbitrary",)),
        name="out_proj",
    )(mix_p, x_p, mix_s, x_s, w_out, norm_w, rw_hi, rw_lo, router_b)


def _route(logits, bm):
    n = logits.shape[0]
    tk = n * TOP_K
    top_val, top_idx = lax.top_k(logits, TOP_K)
    weights = jax.nn.softmax(top_val, axis=-1)
    flat_e = top_idx.reshape(tk).astype(jnp.int32)
    assert tk % ROUTE_TILE == 0
    onehot = (flat_e[:, None] == jnp.arange(N_EXPERTS, dtype=jnp.int32)[None, :])
    onehot = onehot.reshape(tk // ROUTE_TILE, ROUTE_TILE, N_EXPERTS)
    lower = (jnp.arange(ROUTE_TILE)[:, None] > jnp.arange(ROUTE_TILE)[None, :]).astype(BF16)
    within = jnp.einsum('ij,tje->tie', lower, onehot.astype(BF16), preferred_element_type=F32)
    totals = jnp.sum(onehot.astype(F32), axis=1)
    before = jnp.cumsum(totals, axis=0) - totals
    counts = (before[-1] + totals[-1]).astype(jnp.int32)
    padded = (counts + bm - 1) // bm * bm
    pad_end = jnp.cumsum(padded)
    pad_start = pad_end - padded
    row_of = within + before[:, None, :] + pad_start.astype(F32)[None, None, :]
    dest = jnp.sum(jnp.where(onehot, row_of, 0.0), axis=-1).astype(jnp.int32).reshape(tk)
    n_blocks = -(-(tk + N_EXPERTS * (bm - 1)) // bm)
    n_rows = n_blocks * bm
    block_e = jnp.sum(pad_end[None, :] <= (jnp.arange(n_blocks, dtype=jnp.int32) * bm)[:, None], axis=1)
    block_e = jnp.minimum(block_e, N_EXPERTS - 1).astype(jnp.int32)
    asg = jnp.arange(tk, dtype=jnp.int32)
    row_asg = jnp.full((n_rows,), -1, jnp.int32).at[dest].set(asg, unique_indices=True,
                                                              mode='promise_in_bounds')
    is_pad = row_asg < 0
    row_tok = jnp.where(is_pad, 0, row_asg // TOP_K)
    row_slot = jnp.where(is_pad, tk + jnp.arange(n_rows, dtype=jnp.int32) % bm,
                         (row_asg % TOP_K) * n + row_asg // TOP_K)
    n_real = (pad_end[-1] // bm).astype(jnp.int32).reshape(1)
    return weights, row_tok, row_slot, block_e, n_real


def _row_copy(src_ref, dst_ref, sem, src_row, dst_row, n_rows=1):
    return pltpu.make_async_copy(src_ref.at[pl.ds(src_row, n_rows)],
                                 dst_ref.at[pl.ds(dst_row, n_rows)], sem)


def _moe_kernel(be_ref, nreal_ref, idx_ref, idx_next_ref, slot_prev_ref, x_hbm,
                gw_ref, gb_ref, uw_ref, ub_ref, dw_ref, db_ref, ys_hbm,
                acc_ref, xb_ref, xg_ref, yb_ref, gsem, ssem, *, bm, nf):
    blk = pl.program_id(0)
    f = pl.program_id(1)
    n_real = nreal_ref[0]
    cur = blk % 2

    def gather(rows_ref, i):
        return _row_copy(x_hbm, xg_ref, gsem, rows_ref[0, 0, i], i)

    def scatter(i):
        return _row_copy(yb_ref, ys_hbm, ssem, i, slot_prev_ref[0, 0, i])

    def wait_gather():
        _row_copy(x_hbm, xg_ref, gsem, 0, 0, bm).wait()

    def wait_scatter():
        _row_copy(yb_ref, ys_hbm, ssem, 0, 0, bm).wait()

    def start_all(make):
        def body(i, carry):
            make(i).start()
            return carry
        lax.fori_loop(0, bm, body, 0)

    def cast_rows(slot):
        xb_ref[slot] = xg_ref[...].astype(BF16)

    def step(fs):
        if fs == 0:
            @pl.when(blk == 0)
            def _():
                yb_ref[...] = jnp.zeros_like(yb_ref)
                start_all(lambda i: gather(idx_ref, i))
                wait_gather()
                cast_rows(0)

            for i in range(bm):
                gather(idx_next_ref, i).start()
        if fs == nf - 1:
            wait_gather()
            for i in range(bm):
                scatter(i).start()
            cast_rows(1 - cur)

        xb = xb_ref[cur]
        g = jnp.minimum(_dot(xb, gw_ref[0]) + gb_ref[0], SWIGLU_LIMIT)
        u = jnp.clip(_dot(xb, uw_ref[0]) + ub_ref[0], -SWIGLU_LIMIT, SWIGLU_LIMIT)
        hmid = g * jax.nn.sigmoid(SWIGLU_ALPHA * g) * (u + 1.0)
        part = _dot(hmid.astype(BF16), dw_ref[0])
        if fs > 0:
            part = acc_ref[...] + part
        if fs < nf - 1:
            acc_ref[...] = part
        else:
            wait_scatter()
            yb_ref[...] = part + db_ref[0]

    for fs in range(nf):
        pl.when(jnp.logical_and(blk < n_real, f == fs))(functools.partial(step, fs))

    @pl.when(jnp.logical_and(blk == n_real, f == 0))
    def _():
        start_all(scatter)
        wait_scatter()


def _moe_experts(x_all, row_tok, row_slot, block_e, n_real, gate_w, gate_b, up_w, up_b, down_w, down_b,
                 *, bm):
    n_rows = row_tok.shape[0]
    n, d = x_all.shape
    n_blocks = n_rows // bm
    d_ff = gate_w.shape[-1]
    tf = min(MOE_TF, d_ff)
    nf = d_ff // tf
    assert nf >= 2, "the gather started in the first d_ff step is drained in the last one"
    idx = row_tok.reshape(n_blocks, 1, bm)
    trash = TOP_K * n + jnp.arange(bm, dtype=jnp.int32)
    slot_prev = jnp.concatenate([trash, row_slot]).reshape(n_blocks + 1, 1, bm)

    def e_of(b, be, nr):
        return be[jnp.minimum(b, nr[0] - 1)]

    def f_of(b, f, nr):
        return jnp.where(b < nr[0], f, nf - 1)

    smem = functools.partial(pl.BlockSpec, (1, 1, bm), memory_space=pltpu.SMEM)
    return pl.pallas_call(
        functools.partial(_moe_kernel, bm=bm, nf=nf),
        out_shape=jax.ShapeDtypeStruct((TOP_K * n + bm, d), F32),
        grid_spec=pltpu.PrefetchScalarGridSpec(
            num_scalar_prefetch=2,
            grid=(n_blocks, nf),
            in_specs=[
                smem(lambda b, f, be, nr: (b, 0, 0)),
                smem(lambda b, f, be, nr: (jnp.minimum(b + 1, n_blocks - 1), 0, 0)),
                smem(lambda b, f, be, nr: (b, 0, 0)),
                pl.BlockSpec(memory_space=pl.ANY),
                pl.BlockSpec((1, d, tf), lambda b, f, be, nr: (e_of(b, be, nr), 0, f_of(b, f, nr))),
                pl.BlockSpec((1, 1, tf), lambda b, f, be, nr: (e_of(b, be, nr), 0, f_of(b, f, nr))),
                pl.BlockSpec((1, d, tf), lambda b, f, be, nr: (e_of(b, be, nr), 0, f_of(b, f, nr))),
                pl.BlockSpec((1, 1, tf), lambda b, f, be, nr: (e_of(b, be, nr), 0, f_of(b, f, nr))),
                pl.BlockSpec((1, tf, d), lambda b, f, be, nr: (e_of(b, be, nr), f_of(b, f, nr), 0)),
                pl.BlockSpec((1, 1, d), lambda b, f, be, nr: (e_of(b, be, nr), 0, 0)),
            ],
            out_specs=pl.BlockSpec(memory_space=pl.ANY),
            scratch_shapes=[pltpu.VMEM((bm, d), F32), pltpu.VMEM((2, bm, d), BF16),
                            pltpu.VMEM((bm, d), F32), pltpu.VMEM((bm, d), F32),
                            pltpu.SemaphoreType.DMA, pltpu.SemaphoreType.DMA]),
        compiler_params=_cparams(("arbitrary", "arbitrary"), MOE_VMEM_LIMIT),
        name="moe_experts",
    )(block_e, n_real, idx, idx, slot_prev, x_all, gate_w, gate_b, up_w, up_b, down_w, down_b)


def _combine_kernel(*refs):
    y_refs, (w_ref, h_ref, nw_ref, out_ref) = refs[:TOP_K], refs[TOP_K:]
    w = w_ref[...]
    y = w[:, 0:1] * y_refs[0][...]
    for k in range(1, TOP_K):
        y = y + w[:, k:k + 1] * y_refs[k][...]
    out_ref[...] = _rms(h_ref[...] + y) * nw_ref[...]


def _combine(y_slots, weights, h, final_norm_w, *, tok_off):
    n, d = h.shape
    n_all = weights.shape[0]
    tm = min(COMBINE_TM, n)
    ob = tok_off // tm
    slot_blocks = n_all // tm
    y_specs = [pl.BlockSpec((tm, d), functools.partial(lambda k, i: (k * slot_blocks + ob + i, 0), k))
               for k in range(TOP_K)]
    return pl.pallas_call(
        _combine_kernel,
        out_shape=jax.ShapeDtypeStruct((n, d), F32),
        grid=(n // tm,),
        in_specs=[*y_specs,
                  pl.BlockSpec((tm, TOP_K), lambda i: (i + ob, 0)),
                  pl.BlockSpec((tm, d), lambda i: (i, 0)),
                  pl.BlockSpec((1, d), lambda i: (0, 0))],
        out_specs=pl.BlockSpec((tm, d), lambda i: (i, 0)),
        compiler_params=_cparams(("parallel",)),
        name="moe_combine",
    )(*([y_slots] * TOP_K), weights, h, final_norm_w)


def _split_w_in(w_in):
    lr0 = 2 * QA_W + VA_W
    w_main = jnp.concatenate([w_in[:, :lr0], w_in[:, lr0 + GLA_GATE_RANK:]], axis=1).astype(BF16)
    w_lr = jnp.pad(w_in[:, lr0:lr0 + GLA_GATE_RANK], ((0, 0), (0, LR_PAD - GLA_GATE_RANK))).astype(BF16)
    return w_main, w_lr


def kernel(x_prompt, x_sample, state_gla, cache_swa_k, cache_swa_v, attn_norm_w, w_in, gla_gate_up_w, gla_gate_b, gla_out_norm_w, swa_sinks, w_out, ffn_norm_w, router_w, router_b, expert_gate_w, expert_gate_b, expert_up_w, expert_up_b, expert_down_w, expert_down_b, final_norm_w):
    depth = w_in.shape[0]
    assert depth == 1, "the combine applies the final norm, so exactly one layer is supported"
    bp, tp, d = x_prompt.shape
    bs, ts, _ = x_sample.shape
    n_p, n_s = bp * tp, bs * ts
    hp = x_prompt.reshape(n_p, d)
    hs = x_sample.reshape(n_s, d)
    l = 0
    w_main, w_lr = _split_w_in(w_in[l])
    gup = jnp.pad(gla_gate_up_w[l], ((0, LR_PAD - GLA_GATE_RANK), (0, 0))).astype(BF16)
    gate_b = gla_gate_b[l].reshape(1, QA_W)
    onw = gla_out_norm_w[l].reshape(1, GLA_DV)
    anw = attn_norm_w[l].reshape(1, d)
    rw_hi = router_w[l].astype(BF16)
    rw_lo = (router_w[l] - rw_hi.astype(F32)).astype(BF16)

    proj_p, lr_p, kv_p = _in_proj(hp, anw, w_main, w_lr)
    expert_ws = (expert_gate_w[l], expert_up_w[l], expert_down_w[l])
    flat_ws = tuple(w.reshape(-1, w.shape[-1]) for w in expert_ws)
    pieces = n_p // (2 * CHUNK)
    on_side = (len({w.shape for w in flat_ws}) == 1 and pieces >= 2 and tp % (2 * CHUNK) == 0
               and flat_ws[0].shape[0] % pieces == 0 and flat_ws[0].shape[0] // pieces <= CAST_MAX_ROWS)
    mix_p, st_p, *flat_wb = _mix_prompt(proj_p, lr_p, gup, gate_b, onw, swa_sinks[l],
                                        flat_ws if on_side else (), batch=bp, seq=tp)
    if not on_side:
        flat_wb = [w.astype(BF16) for w in flat_ws]
    gate_w16, up_w16, down_w16 = (wb.reshape(w.shape) for wb, w in zip(flat_wb, expert_ws))
    proj_s, lr_s, kv_s = _in_proj(hs, anw, w_main, w_lr)
    mix_s, st_s = _mix_sample(proj_s, lr_s, gup, gate_b, onw, swa_sinks[l],
                              jnp.swapaxes(state_gla[l], -1, -2),
                              cache_swa_k[l].reshape(bs, WINDOW, KB_W),
                              cache_swa_v[l].reshape(bs, WINDOW, KB_W), batch=bs, seq=ts)

    hp, hs, xn_all, lg_all = _out_proj(mix_p, hp, mix_s, hs, w_out[l].astype(BF16),
                                       ffn_norm_w[l].reshape(1, d), rw_hi, rw_lo,
                                       router_b[l].reshape(1, N_EXPERTS))

    weights, row_tok, row_slot, block_e, n_real = _route(lg_all, MOE_BM)
    y_slots = _moe_experts(xn_all, row_tok, row_slot, block_e, n_real,
                          gate_w16, expert_gate_b[l][:, None, :], up_w16, expert_up_b[l][:, None, :],
                          down_w16, expert_down_b[l][:, None, :], bm=MOE_BM)
    fin = final_norm_w.reshape(1, d)
    y_prompt = _combine(y_slots, weights, hp, fin, tok_off=0).reshape(bp, tp, d)
    y_sample = _combine(y_slots, weights, hs, fin, tok_off=n_p).reshape(bs, ts, d)

    kvp = kv_p.reshape(bp, tp, 2 * KB_W)[:, -WINDOW:].reshape(bp, WINDOW, 2, SWA_KV_HEADS, HEAD_DIM)
    kvs = kv_s.reshape(bs, ts, 2, SWA_KV_HEADS, HEAD_DIM)
    return (y_prompt, y_sample,
            jnp.swapaxes(st_p, -1, -2)[None], kvp[:, :, 0][None], kvp[:, :, 1][None],
            jnp.swapaxes(st_s, -1, -2)[None],
            jnp.concatenate([cache_swa_k[l], kvs[:, :, 0]], axis=1)[:, -WINDOW:][None],
            jnp.concatenate([cache_swa_v[l], kvs[:, :, 1]], axis=1)[:, -WINDOW:][None])
```

```python
import functools

import jax
import jax.numpy as jnp
from jax import lax
from jax.experimental import pallas as pl
from jax.experimental.pallas import tpu as pltpu

F32 = jnp.float32
BF16 = jnp.bfloat16

CHUNK = 64
GLA_HEADS = 4
GLA_DK = 128
GLA_DV = 256
GLA_GATE_RANK = 16
GLA_GATE_NORM = 16.0
SWA_Q_HEADS = 16
SWA_KV_HEADS = 4
SWA_GROUP = SWA_Q_HEADS // SWA_KV_HEADS
HEAD_DIM = 64
WINDOW = 128
WINDOW_CHUNKS = WINDOW // CHUNK
ALIBI_MAX_BIAS = 8.0
N_EXPERTS = 32
TOP_K = 4
SWIGLU_LIMIT = 7.0
SWIGLU_ALPHA = 1.702
RMS_EPS = 1e-5
NEG_INF = -1e30

QA_W = GLA_HEADS * GLA_DK
VA_W = GLA_HEADS * GLA_DV
QB_W = SWA_Q_HEADS * HEAD_DIM
KB_W = SWA_KV_HEADS * HEAD_DIM
MIX_W = VA_W + QB_W
MAIN_W = 2 * QA_W + 2 * VA_W + QB_W + 2 * KB_W
QB_COL = (2 * QA_W + 2 * VA_W) // QB_W
KB_COL = (2 * QA_W + 2 * VA_W + QB_W) // KB_W
LANE = 128
LR_PAD = LANE

PROJ_TN = 1536
PROJ_TM = 1024
ROW_TILE = 512
MOE_BM = 512
MOE_TF = 1024
CAST_MAX_ROWS = 256
OUT_TM = 256
COMBINE_TM = 256
ROUTE_TILE = 512
VMEM_LIMIT = 48 * 1024 * 1024
LARGE_VMEM_LIMIT = 58 * 1024 * 1024


def _cparams(sem, vmem=VMEM_LIMIT):
    return pltpu.CompilerParams(dimension_semantics=sem, vmem_limit_bytes=vmem)


def _dot(a, b):
    return jnp.dot(a, b, preferred_element_type=F32)


def _dot_tn(a, b):
    return lax.dot_general(a, b, (((0,), (0,)), ((), ())), preferred_element_type=F32)


def _dot_nt(a, b):
    return lax.dot_general(a, b, (((1,), (1,)), ((), ())), preferred_element_type=F32)


def _rms(x):
    return x * lax.rsqrt(jnp.mean(x * x, axis=-1, keepdims=True) + RMS_EPS)


def _hi_lo(x):
    hi = x.astype(BF16)
    return hi, (x - hi.astype(F32)).astype(BF16)


def _in_proj_kernel(x_ref, nw_ref, w_ref, wlr_ref, out_ref, lr_ref, kv_ref, xn_ref, *, n_col_steps):
    j = pl.program_id(1)

    @pl.when(j == 0)
    def _():
        xnb = (_rms(x_ref[...]) * nw_ref[...]).astype(BF16)
        xn_ref[...] = xnb
        lr_ref[...] = _dot(xnb, wlr_ref[...]).astype(BF16)

    acc = _dot(xn_ref[...], w_ref[...])
    out_ref[...] = acc.astype(BF16)

    @pl.when(j == n_col_steps - 1)
    def _():
        kv_ref[...] = acc[:, PROJ_TN - 2 * KB_W:]


def _in_proj(x2d, norm_w, w_main, w_lr):
    n, d = x2d.shape
    tm = min(PROJ_TM, n)
    n_col_steps = MAIN_W // PROJ_TN
    return pl.pallas_call(
        functools.partial(_in_proj_kernel, n_col_steps=n_col_steps),
        out_shape=(jax.ShapeDtypeStruct((n, MAIN_W), BF16),
                   jax.ShapeDtypeStruct((n, LR_PAD), BF16),
                   jax.ShapeDtypeStruct((n, 2 * KB_W), F32)),
        grid=(n // tm, n_col_steps),
        in_specs=[pl.BlockSpec((tm, d), lambda i, j: (i, 0)),
                  pl.BlockSpec((1, d), lambda i, j: (0, 0)),
                  pl.BlockSpec((d, PROJ_TN), lambda i, j: (0, j)),
                  pl.BlockSpec((d, LR_PAD), lambda i, j: (0, 0))],
        out_specs=(pl.BlockSpec((tm, PROJ_TN), lambda i, j: (i, j)),
                   pl.BlockSpec((tm, LR_PAD), lambda i, j: (i, 0)),
                   pl.BlockSpec((tm, 2 * KB_W), lambda i, j: (i, 0))),
        scratch_shapes=[pltpu.VMEM((tm, d), BF16)],
        compiler_params=_cparams(("parallel", "arbitrary"), LARGE_VMEM_LIMIT),
        name="in_proj",
    )(x2d, norm_w, w_main, w_lr)


def _gla_chunk(rows, chunk, q_ref, k_ref, v_ref, ga_ref, lr_ref, gup, gb, nw, tri, s_scr, o_ref):
    z = _dot(lr_ref[rows, :], gup) + gb
    fc = (jnp.minimum(z, 0.0) - jnp.log1p(jnp.exp(-jnp.abs(z)))) * (1.0 / GLA_GATE_NORM)
    hi, lo = _hi_lo(fc)
    g = _dot(tri, hi) + _dot(tri, lo)
    g_end = g[chunk - 1:chunk, :]
    k_dec = (k_ref[rows, :].astype(F32) * jnp.exp(g_end - g)).astype(BF16)
    a_end = jnp.exp(g_end)
    for h in range(GLA_HEADS):
        ks = slice(h * GLA_DK, (h + 1) * GLA_DK)
        vs = slice(h * GLA_DV, (h + 1) * GLA_DV)
        upd = _dot_tn(v_ref[rows, vs], k_dec[:, ks])
        s_new = s_scr[h] * a_end[:, ks] + upd
        s_scr[h] = s_new
        o = _dot_nt(q_ref[rows, ks], s_new.astype(BF16)) * (GLA_DK ** -0.5)
        gate = ga_ref[rows, vs].astype(F32)
        o_ref[rows, vs] = (_rms(o) * nw * (gate * jax.nn.sigmoid(gate))).astype(BF16)


def _alibi_slopes():
    s = jnp.exp2(-ALIBI_MAX_BIAS * jnp.arange(1, SWA_Q_HEADS + 1, dtype=F32) / SWA_Q_HEADS)
    return s.reshape(SWA_KV_HEADS, SWA_GROUP)


def _stacked_bias(q_off, lq, lk):
    dist = jnp.abs(q_off + jnp.arange(lq)[:, None] - jnp.arange(lk)[None, :]).astype(F32)
    b = -_alibi_slopes()[:, :, None, None] * dist
    return b.reshape(SWA_KV_HEADS, SWA_GROUP * lq, lk)


def _stacked_sinks(sinks, lq):
    s = sinks.astype(F32).reshape(SWA_KV_HEADS, SWA_GROUP, 1)
    return jnp.broadcast_to(s, (SWA_KV_HEADS, SWA_GROUP, lq)).reshape(SWA_KV_HEADS, SWA_GROUP * lq, 1)


def _attend_group(q, keys, vals, bias, sink, valid):
    lq = q.shape[0]
    qs = jnp.concatenate([q[:, j * HEAD_DIM:(j + 1) * HEAD_DIM] for j in range(SWA_GROUP)], axis=0)
    s = _dot_nt(qs, keys) * (HEAD_DIM ** -0.5) + bias
    if valid is not None:
        s = jnp.where(valid, s, NEG_INF)
    m = jnp.maximum(jnp.max(s, axis=-1, keepdims=True), sink)
    p = jnp.exp(s - m)
    denom = jnp.sum(p, axis=-1, keepdims=True) + jnp.exp(sink - m)
    o = _dot(p.astype(BF16), vals) / denom
    return jnp.concatenate([o[j * lq:(j + 1) * lq, :] for j in range(SWA_GROUP)], axis=1)


def _mix_prompt_kernel(*refs, n_chunks, n_steps, n_cast, cast_rows):
    (q_ref, k_ref, v_ref, ga_ref, lr_ref, qb_ref, kc_ref, vc_ref, kp_ref, vp_ref,
     gup_ref, gb_ref, nw_ref, tri_ref, bias_ref, sink_ref) = refs[:16]
    w_hbm = refs[16:16 + n_cast]
    o_ref, st_ref = refs[16 + n_cast:18 + n_cast]
    wb_hbm = refs[18 + n_cast:18 + 2 * n_cast]
    s_scr, k_scr, v_scr = refs[18 + 2 * n_cast:21 + 2 * n_cast]
    if n_cast:
        win, wout, isem, osem = refs[21 + 2 * n_cast:]
    t = pl.program_id(1)
    back = WINDOW_CHUNKS * CHUNK
    span = back + CHUNK
    pairs = n_chunks // 2
    step_id = pl.program_id(0) * pl.num_programs(1) + t
    total = n_steps * pairs

    def w_in(j, piece, slot):
        return pltpu.make_async_copy(w_hbm[j].at[pl.ds(piece * cast_rows, cast_rows)],
                                     win.at[slot, j], isem.at[slot])

    def w_out(j, piece, slot):
        return pltpu.make_async_copy(wout.at[slot, j],
                                     wb_hbm[j].at[pl.ds(piece * cast_rows, cast_rows)], osem.at[slot])

    @pl.when(t == 0)
    def _():
        s_scr[...] = jnp.zeros_like(s_scr)

    k_scr[0:back, :] = kp_ref[...]
    v_scr[0:back, :] = vp_ref[...]
    k_scr[back:, :] = kc_ref[...]
    v_scr[back:, :] = vc_ref[...]
    key_idx = lax.broadcasted_iota(jnp.int32, (SWA_GROUP * CHUNK, span), 1)
    gup, gb, nw, tri = gup_ref[...], gb_ref[...], nw_ref[...], tri_ref[...]

    def chunk_step(c):
        r0 = pl.multiple_of(c * CHUNK, CHUNK)
        rows = pl.ds(r0, CHUNK)
        _gla_chunk(rows, CHUNK, q_ref, k_ref, v_ref, ga_ref, lr_ref, gup, gb, nw, tri, s_scr, o_ref)
        valid = jnp.logical_or(t > 0, key_idx >= back - c * CHUNK)
        for g in range(SWA_KV_HEADS):
            hs = slice(g * HEAD_DIM, (g + 1) * HEAD_DIM)
            qs = slice(g * SWA_GROUP * HEAD_DIM, (g + 1) * SWA_GROUP * HEAD_DIM)
            o = _attend_group(qb_ref[rows, qs], k_scr[pl.ds(r0, span), hs],
                              v_scr[pl.ds(r0, span), hs], bias_ref[g], sink_ref[g], valid)
            o_ref[rows, VA_W + g * SWA_GROUP * HEAD_DIM:VA_W + (g + 1) * SWA_GROUP * HEAD_DIM] = o.astype(BF16)

    def pair_step(p, carry):
        piece = step_id * pairs + p
        slot = piece % 2
        if n_cast:
            @pl.when(piece == 0)
            def _():
                for j in range(n_cast):
                    w_in(j, 0, 0).start()

            for j in range(n_cast):
                w_in(j, piece, slot).wait()

            @pl.when(piece + 1 < total)
            def _():
                for j in range(n_cast):
                    w_in(j, piece + 1, 1 - slot).start()

            @pl.when(piece >= 2)
            def _():
                for j in range(n_cast):
                    w_out(j, piece - 2, slot).wait()

        chunk_step(2 * p)
        chunk_step(2 * p + 1)
        if n_cast:
            for j in range(n_cast):
                wout[slot, j] = win[slot, j].astype(BF16)
            for j in range(n_cast):
                w_out(j, piece, slot).start()
        return carry

    lax.fori_loop(0, pairs, pair_step, 0)

    @pl.when(t == pl.num_programs(1) - 1)
    def _():
        st_ref[0] = s_scr[...]

    if n_cast:
        @pl.when(step_id == n_steps - 1)
        def _():
            for piece in (total - 2, total - 1):
                for j in range(n_cast):
                    w_out(j, piece, piece % 2).wait()


def _gla_consts(gup, gate_b, norm_w, chunk):
    tri = (jnp.arange(chunk)[:, None] >= jnp.arange(chunk)[None, :]).astype(BF16)
    specs = [pl.BlockSpec((LR_PAD, QA_W), lambda *_: (0, 0)),
             pl.BlockSpec((1, QA_W), lambda *_: (0, 0)),
             pl.BlockSpec((1, GLA_DV), lambda *_: (0, 0)),
             pl.BlockSpec((chunk, chunk), lambda *_: (0, 0))]
    return specs, [gup, gate_b, norm_w, tri]


def _mix_prompt(proj, lr, gup, gate_b, norm_w, sinks, cast_ws=(), *, batch, seq):
    rows = min(ROW_TILE, seq)
    n_chunks = rows // CHUNK
    nt = seq // rows
    back = WINDOW_CHUNKS * CHUNK
    pb = rows // back
    bias = _stacked_bias(back, CHUNK, back + CHUNK)
    sink = _stacked_sinks(sinks, CHUNK)
    rb = lambda b, t: b * nt + t
    prev = lambda b, t: jnp.maximum(rb(b, t) * pb - 1, 0)
    cspecs, cargs = _gla_consts(gup, gate_b, norm_w, CHUNK)
    n_cast = len(cast_ws)
    cast_rows, cast_scratch = 0, []
    if n_cast:
        assert n_chunks % 2 == 0
        w_rows, w_cols = cast_ws[0].shape
        pieces = batch * nt * (n_chunks // 2)
        assert all(w.shape == (w_rows, w_cols) for w in cast_ws) and w_rows % pieces == 0 and pieces >= 2
        cast_rows = w_rows // pieces
        cast_scratch = [pltpu.VMEM((2, n_cast, cast_rows, w_cols), F32),
                        pltpu.VMEM((2, n_cast, cast_rows, w_cols), BF16),
                        pltpu.SemaphoreType.DMA((2,)), pltpu.SemaphoreType.DMA((2,))]
    any_spec = pl.BlockSpec(memory_space=pl.ANY)
    return pl.pallas_call(
        functools.partial(_mix_prompt_kernel, n_chunks=n_chunks, n_steps=batch * nt, n_cast=n_cast,
                          cast_rows=cast_rows),
        out_shape=(jax.ShapeDtypeStruct((batch * seq, MIX_W), BF16),
                   jax.ShapeDtypeStruct((batch, GLA_HEADS, GLA_DV, GLA_DK), F32),
                   *[jax.ShapeDtypeStruct(w.shape, BF16) for w in cast_ws]),
        grid=(batch, nt),
        in_specs=[pl.BlockSpec((rows, QA_W), lambda b, t: (rb(b, t), 0)),
                  pl.BlockSpec((rows, QA_W), lambda b, t: (rb(b, t), 1)),
                  pl.BlockSpec((rows, VA_W), lambda b, t: (rb(b, t), 1)),
                  pl.BlockSpec((rows, VA_W), lambda b, t: (rb(b, t), 2)),
                  pl.BlockSpec((rows, LR_PAD), lambda b, t: (rb(b, t), 0)),
                  pl.BlockSpec((rows, QB_W), lambda b, t: (rb(b, t), QB_COL)),
                  pl.BlockSpec((rows, KB_W), lambda b, t: (rb(b, t), KB_COL)),
                  pl.BlockSpec((rows, KB_W), lambda b, t: (rb(b, t), KB_COL + 1)),
                  pl.BlockSpec((back, KB_W), lambda b, t: (prev(b, t), KB_COL)),
                  pl.BlockSpec((back, KB_W), lambda b, t: (prev(b, t), KB_COL + 1)),
                  *cspecs,
                  pl.BlockSpec(bias.shape, lambda b, t: (0, 0, 0)),
                  pl.BlockSpec(sink.shape, lambda b, t: (0, 0, 0)),
                  *[any_spec] * n_cast],
        out_specs=(pl.BlockSpec((rows, MIX_W), lambda b, t: (rb(b, t), 0)),
                   pl.BlockSpec((1, GLA_HEADS, GLA_DV, GLA_DK), lambda b, t: (b, 0, 0, 0)),
                   *[any_spec] * n_cast),
        scratch_shapes=[pltpu.VMEM((GLA_HEADS, GLA_DV, GLA_DK), F32),
                        pltpu.VMEM((rows + back, KB_W), BF16),
                        pltpu.VMEM((rows + back, KB_W), BF16),
                        *cast_scratch],
        compiler_params=_cparams(("arbitrary", "arbitrary")),
        name="mix_prompt",
    )(proj, proj, proj, proj, lr, proj, proj, proj, proj, proj, *cargs, bias, sink, *cast_ws)


def _mix_sample_kernel(q_ref, k_ref, v_ref, ga_ref, lr_ref, qb_ref, kn_ref, vn_ref, wk_ref, wv_ref,
                       s0_ref, gup_ref, gb_ref, nw_ref, tri_ref, bias_ref, sink_ref,
                       o_ref, st_ref, s_scr, *, seq):
    s_scr[...] = s0_ref[0]
    _gla_chunk(slice(0, seq), seq, q_ref, k_ref, v_ref, ga_ref, lr_ref,
               gup_ref[...], gb_ref[...], nw_ref[...], tri_ref[...], s_scr, o_ref)
    st_ref[0] = s_scr[...]
    keys = jnp.concatenate([wk_ref[0].astype(BF16), kn_ref[...]], axis=0)
    vals = jnp.concatenate([wv_ref[0].astype(BF16), vn_ref[...]], axis=0)
    for g in range(SWA_KV_HEADS):
        hs = slice(g * HEAD_DIM, (g + 1) * HEAD_DIM)
        qs = slice(g * SWA_GROUP * HEAD_DIM, (g + 1) * SWA_GROUP * HEAD_DIM)
        o = _attend_group(qb_ref[:, qs], keys[:, hs], vals[:, hs], bias_ref[g], sink_ref[g], None)
        o_ref[:, VA_W + g * SWA_GROUP * HEAD_DIM:VA_W + (g + 1) * SWA_GROUP * HEAD_DIM] = o.astype(BF16)


def _mix_sample(proj, lr, gup, gate_b, norm_w, sinks, s0_t, win_k, win_v, *, batch, seq):
    bias = _stacked_bias(WINDOW, seq, WINDOW + seq)
    sink = _stacked_sinks(sinks, seq)
    cspecs, cargs = _gla_consts(gup, gate_b, norm_w, seq)
    state_spec = pl.BlockSpec((1, GLA_HEADS, GLA_DV, GLA_DK), lambda b: (b, 0, 0, 0))
    return pl.pallas_call(
        functools.partial(_mix_sample_kernel, seq=seq),
        out_shape=(jax.ShapeDtypeStruct((batch * seq, MIX_W), BF16),
                   jax.ShapeDtypeStruct((batch, GLA_HEADS, GLA_DV, GLA_DK), F32)),
        grid=(batch,),
        in_specs=[pl.BlockSpec((seq, QA_W), lambda b: (b, 0)),
                  pl.BlockSpec((seq, QA_W), lambda b: (b, 1)),
                  pl.BlockSpec((seq, VA_W), lambda b: (b, 1)),
                  pl.BlockSpec((seq, VA_W), lambda b: (b, 2)),
                  pl.BlockSpec((seq, LR_PAD), lambda b: (b, 0)),
                  pl.BlockSpec((seq, QB_W), lambda b: (b, QB_COL)),
                  pl.BlockSpec((seq, KB_W), lambda b: (b, KB_COL)),
                  pl.BlockSpec((seq, KB_W), lambda b: (b, KB_COL + 1)),
                  pl.BlockSpec((1, WINDOW, KB_W), lambda b: (b, 0, 0)),
                  pl.BlockSpec((1, WINDOW, KB_W), lambda b: (b, 0, 0)),
                  state_spec,
                  *cspecs,
                  pl.BlockSpec(bias.shape, lambda b: (0, 0, 0)),
                  pl.BlockSpec(sink.shape, lambda b: (0, 0, 0))],
        out_specs=(pl.BlockSpec((seq, MIX_W), lambda b: (b, 0)), state_spec),
        scratch_shapes=[pltpu.VMEM((GLA_HEADS, GLA_DV, GLA_DK), F32)],
        compiler_params=_cparams(("parallel",)),
        name="mix_sample",
    )(proj, proj, proj, proj, lr, proj, proj, proj, win_k, win_v, s0_t, *cargs, bias, sink)


def _out_proj_kernel(mp_ref, xp_ref, ms_ref, xs_ref, w_ref, nw_ref, rwh_ref, rwl_ref, rb_ref,
                     hp_ref, hs_ref, xn_ref, lg_ref, *, p_tiles):
    i = pl.program_id(0)

    def tile(m_ref, x_ref, h_ref):
        h = x_ref[...] + _dot(m_ref[...], w_ref[...])
        h_ref[...] = h
        xn = _rms(h) * nw_ref[...]
        xn_ref[...] = xn
        xh, xl = _hi_lo(xn)
        rwh = rwh_ref[...]
        lg_ref[...] = _dot(xh, rwh) + _dot(xl, rwh) + _dot(xh, rwl_ref[...]) + rb_ref[...]

    @pl.when(i < p_tiles)
    def _():
        tile(mp_ref, xp_ref, hp_ref)

    @pl.when(i >= p_tiles)
    def _():
        tile(ms_ref, xs_ref, hs_ref)


def _out_proj(mix_p, x_p, mix_s, x_s, w_out, norm_w, rw_hi, rw_lo, router_b):
    n_p, d = x_p.shape
    n_s = x_s.shape[0]
    tm = min(OUT_TM, n_p, n_s)
    p_tiles, s_tiles = n_p // tm, n_s // tm
    n_all = n_p + n_s
    pidx = lambda i: (jnp.minimum(i, p_tiles - 1), 0)
    sidx = lambda i: (jnp.maximum(i - p_tiles, 0), 0)
    const = lambda i: (0, 0)
    return pl.pallas_call(
        functools.partial(_out_proj_kernel, p_tiles=p_tiles),
        out_shape=(jax.ShapeDtypeStruct((n_p, d), F32),
                   jax.ShapeDtypeStruct((n_s, d), F32),
                   jax.ShapeDtypeStruct((n_all, d), F32),
                   jax.ShapeDtypeStruct((n_all, N_EXPERTS), F32)),
        grid=(p_tiles + s_tiles,),
        in_specs=[pl.BlockSpec((tm, MIX_W), pidx),
                  pl.BlockSpec((tm, d), pidx),
                  pl.BlockSpec((tm, MIX_W), sidx),
                  pl.BlockSpec((tm, d), sidx),
                  pl.BlockSpec((MIX_W, d), const),
                  pl.BlockSpec((1, d), const),
                  pl.BlockSpec((d, N_EXPERTS), const),
                  pl.BlockSpec((d, N_EXPERTS), const),
                  pl.BlockSpec((1, N_EXPERTS), const)],
        out_specs=(pl.BlockSpec((tm, d), pidx),
                   pl.BlockSpec((tm, d), sidx),
                   pl.BlockSpec((tm, d), lambda i: (i, 0)),
                   pl.BlockSpec((tm, N_EXPERTS), lambda i: (i, 0))),
        compiler_params=_cparams(("arbitrary",)),
        name="out_proj",
    )(mix_p, x_p, mix_s, x_s, w_out, norm_w, rw_hi, rw_lo, router_b)


def _route(logits, bm):
    n = logits.shape[0]
    tk = n * TOP_K
    top_val, top_idx = lax.top_k(logits, TOP_K)
    weights = jax.nn.softmax(top_val, axis=-1)
    flat_e = top_idx.reshape(tk).astype(jnp.int32)
    assert tk % ROUTE_TILE == 0
    onehot = (flat_e[:, None] == jnp.arange(N_EXPERTS, dtype=jnp.int32)[None, :])
    onehot = onehot.reshape(tk // ROUTE_TILE, ROUTE_TILE, N_EXPERTS)
    lower = (jnp.arange(ROUTE_TILE)[:, None] > jnp.arange(ROUTE_TILE)[None, :]).astype(BF16)
    within = jnp.einsum('ij,tje->tie', lower, onehot.astype(BF16), preferred_element_type=F32)
    totals = jnp.sum(onehot.astype(F32), axis=1)
    before = jnp.cumsum(totals, axis=0) - totals
    counts = (before[-1] + totals[-1]).astype(jnp.int32)
    padded = (counts + bm - 1) // bm * bm
    pad_end = jnp.cumsum(padded)
    pad_start = pad_end - padded
    row_of = within + before[:, None, :] + pad_start.astype(F32)[None, None, :]
    dest = jnp.sum(jnp.where(onehot, row_of, 0.0), axis=-1).astype(jnp.int32).reshape(tk)
    n_blocks = -(-(tk + N_EXPERTS * (bm - 1)) // bm)
    n_rows = n_blocks * bm
    block_e = jnp.sum(pad_end[None, :] <= (jnp.arange(n_blocks, dtype=jnp.int32) * bm)[:, None], axis=1)
    block_e = jnp.minimum(block_e, N_EXPERTS - 1).astype(jnp.int32)
    asg = jnp.arange(tk, dtype=jnp.int32)
    row_asg = jnp.full((n_rows,), -1, jnp.int32).at[dest].set(asg, unique_indices=True,
                                                              mode='promise_in_bounds')
    is_pad = row_asg < 0
    row_tok = jnp.where(is_pad, 0, row_asg // TOP_K)
    row_slot = jnp.where(is_pad, tk + jnp.arange(n_rows, dtype=jnp.int32) % bm,
                         (row_asg % TOP_K) * n + row_asg // TOP_K)
    n_real = (pad_end[-1] // bm).astype(jnp.int32).reshape(1)
    return weights, row_tok, row_slot, block_e, n_real


def _row_copy(src_ref, dst_ref, sem, src_row, dst_row, n_rows=1):
    return pltpu.make_async_copy(src_ref.at[pl.ds(src_row, n_rows)],
                                 dst_ref.at[pl.ds(dst_row, n_rows)], sem)


def _moe_kernel(be_ref, nreal_ref, idx_ref, idx_next_ref, slot_prev_ref, x_hbm,
                gw_ref, gb_ref, uw_ref, ub_ref, dw_ref, db_ref, ys_hbm,
                acc_ref, xb_ref, xg_ref, yb_ref, gsem, ssem, *, bm, nf):
    blk = pl.program_id(0)
    f = pl.program_id(1)
    n_real = nreal_ref[0]
    cur = blk % 2

    def gather(rows_ref, i):
        return _row_copy(x_hbm, xg_ref, gsem, rows_ref[0, 0, i], i)

    def scatter(i):
        return _row_copy(yb_ref, ys_hbm, ssem, i, slot_prev_ref[0, 0, i])

    def wait_gather():
        _row_copy(x_hbm, xg_ref, gsem, 0, 0, bm).wait()

    def wait_scatter():
        _row_copy(yb_ref, ys_hbm, ssem, 0, 0, bm).wait()

    def start_all(make):
        def body(i, carry):
            make(i).start()
            return carry
        lax.fori_loop(0, bm, body, 0)

    def cast_rows(slot):
        xb_ref[slot] = xg_ref[...].astype(BF16)

    def step(fs):
        if fs == 0:
            @pl.when(blk == 0)
            def _():
                yb_ref[...] = jnp.zeros_like(yb_ref)
                start_all(lambda i: gather(idx_ref, i))
                wait_gather()
                cast_rows(0)

            for i in range(bm):
                gather(idx_next_ref, i).start()
        if fs == nf - 1:
            wait_gather()
            for i in range(bm):
                scatter(i).start()
            cast_rows(1 - cur)

        xb = xb_ref[cur]
        g = jnp.minimum(_dot(xb, gw_ref[0]) + gb_ref[0], SWIGLU_LIMIT)
        u = jnp.clip(_dot(xb, uw_ref[0]) + ub_ref[0], -SWIGLU_LIMIT, SWIGLU_LIMIT)
        hmid = g * jax.nn.sigmoid(SWIGLU_ALPHA * g) * (u + 1.0)
        part = _dot(hmid.astype(BF16), dw_ref[0])
        if fs > 0:
            part = acc_ref[...] + part
        if fs < nf - 1:
            acc_ref[...] = part
        else:
            wait_scatter()
            yb_ref[...] = part + db_ref[0]

    for fs in range(nf):
        pl.when(jnp.logical_and(blk < n_real, f == fs))(functools.partial(step, fs))

    @pl.when(jnp.logical_and(blk == n_real, f == 0))
    def _():
        start_all(scatter)
        wait_scatter()


def _moe_experts(x_all, row_tok, row_slot, block_e, n_real, gate_w, gate_b, up_w, up_b, down_w, down_b,
                 *, bm):
    n_rows = row_tok.shape[0]
    n, d = x_all.shape
    n_blocks = n_rows // bm
    d_ff = gate_w.shape[-1]
    tf = min(MOE_TF, d_ff)
    nf = d_ff // tf
    assert nf >= 2, "the gather started in the first d_ff step is drained in the last one"
    idx = row_tok.reshape(n_blocks, 1, bm)
    trash = TOP_K * n + jnp.arange(bm, dtype=jnp.int32)
    slot_prev = jnp.concatenate([trash, row_slot]).reshape(n_blocks + 1, 1, bm)

    def e_of(b, be, nr):
        return be[jnp.minimum(b, nr[0] - 1)]

    def f_of(b, f, nr):
        return jnp.where(b < nr[0], f, nf - 1)

    smem = functools.partial(pl.BlockSpec, (1, 1, bm), memory_space=pltpu.SMEM)
    return pl.pallas_call(
        functools.partial(_moe_kernel, bm=bm, nf=nf),
        out_shape=jax.ShapeDtypeStruct((TOP_K * n + bm, d), F32),
        grid_spec=pltpu.PrefetchScalarGridSpec(
            num_scalar_prefetch=2,
            grid=(n_blocks, nf),
            in_specs=[
                smem(lambda b, f, be, nr: (b, 0, 0)),
                smem(lambda b, f, be, nr: (jnp.minimum(b + 1, n_blocks - 1), 0, 0)),
                smem(lambda b, f, be, nr: (b, 0, 0)),
                pl.BlockSpec(memory_space=pl.ANY),
                pl.BlockSpec((1, d, tf), lambda b, f, be, nr: (e_of(b, be, nr), 0, f_of(b, f, nr))),
                pl.BlockSpec((1, 1, tf), lambda b, f, be, nr: (e_of(b, be, nr), 0, f_of(b, f, nr))),
                pl.BlockSpec((1, d, tf), lambda b, f, be, nr: (e_of(b, be, nr), 0, f_of(b, f, nr))),
                pl.BlockSpec((1, 1, tf), lambda b, f, be, nr: (e_of(b, be, nr), 0, f_of(b, f, nr))),
                pl.BlockSpec((1, tf, d), lambda b, f, be, nr: (e_of(b, be, nr), f_of(b, f, nr), 0)),
                pl.BlockSpec((1, 1, d), lambda b, f, be, nr: (e_of(b, be, nr), 0, 0)),
            ],
            out_specs=pl.BlockSpec(memory_space=pl.ANY),
            scratch_shapes=[pltpu.VMEM((bm, d), F32), pltpu.VMEM((2, bm, d), BF16),
                            pltpu.VMEM((bm, d), F32), pltpu.VMEM((bm, d), F32),
                            pltpu.SemaphoreType.DMA, pltpu.SemaphoreType.DMA]),
        compiler_params=_cparams(("arbitrary", "arbitrary"), LARGE_VMEM_LIMIT),
        name="moe_experts",
    )(block_e, n_real, idx, idx, slot_prev, x_all, gate_w, gate_b, up_w, up_b, down_w, down_b)


def _combine_kernel(*refs):
    y_refs, (w_ref, h_ref, nw_ref, out_ref) = refs[:TOP_K], refs[TOP_K:]
    w = w_ref[...]
    y = w[:, 0:1] * y_refs[0][...]
    for k in range(1, TOP_K):
        y = y + w[:, k:k + 1] * y_refs[k][...]
    out_ref[...] = _rms(h_ref[...] + y) * nw_ref[...]


def _combine(y_slots, weights, h, final_norm_w, *, tok_off):
    n, d = h.shape
    n_all = weights.shape[0]
    tm = min(COMBINE_TM, n)
    ob = tok_off // tm
    slot_blocks = n_all // tm
    y_specs = [pl.BlockSpec((tm, d), functools.partial(lambda k, i: (k * slot_blocks + ob + i, 0), k))
               for k in range(TOP_K)]
    return pl.pallas_call(
        _combine_kernel,
        out_shape=jax.ShapeDtypeStruct((n, d), F32),
        grid=(n // tm,),
        in_specs=[*y_specs,
                  pl.BlockSpec((tm, TOP_K), lambda i: (i + ob, 0)),
                  pl.BlockSpec((tm, d), lambda i: (i, 0)),
                  pl.BlockSpec((1, d), lambda i: (0, 0))],
        out_specs=pl.BlockSpec((tm, d), lambda i: (i, 0)),
        compiler_params=_cparams(("parallel",)),
        name="moe_combine",
    )(*([y_slots] * TOP_K), weights, h, final_norm_w)


def _split_w_in(w_in):
    lr0 = 2 * QA_W + VA_W
    w_main = jnp.concatenate([w_in[:, :lr0], w_in[:, lr0 + GLA_GATE_RANK:]], axis=1).astype(BF16)
    w_lr = jnp.pad(w_in[:, lr0:lr0 + GLA_GATE_RANK], ((0, 0), (0, LR_PAD - GLA_GATE_RANK))).astype(BF16)
    return w_main, w_lr


def kernel(x_prompt, x_sample, state_gla, cache_swa_k, cache_swa_v, attn_norm_w, w_in, gla_gate_up_w, gla_gate_b, gla_out_norm_w, swa_sinks, w_out, ffn_norm_w, router_w, router_b, expert_gate_w, expert_gate_b, expert_up_w, expert_up_b, expert_down_w, expert_down_b, final_norm_w):
    depth = w_in.shape[0]
    assert depth == 1, "the combine applies the final norm, so exactly one layer is supported"
    bp, tp, d = x_prompt.shape
    bs, ts, _ = x_sample.shape
    n_p, n_s = bp * tp, bs * ts
    hp = x_prompt.reshape(n_p, d)
    hs = x_sample.reshape(n_s, d)
    l = 0
    w_main, w_lr = _split_w_in(w_in[l])
    gup = jnp.pad(gla_gate_up_w[l], ((0, LR_PAD - GLA_GATE_RANK), (0, 0))).astype(BF16)
    gate_b = gla_gate_b[l].reshape(1, QA_W)
    onw = gla_out_norm_w[l].reshape(1, GLA_DV)
    anw = attn_norm_w[l].reshape(1, d)
    rw_hi = router_w[l].astype(BF16)
    rw_lo = (router_w[l] - rw_hi.astype(F32)).astype(BF16)

    proj_p, lr_p, kv_p = _in_proj(hp, anw, w_main, w_lr)
    expert_ws = (expert_gate_w[l], expert_up_w[l], expert_down_w[l])
    flat_ws = tuple(w.reshape(-1, w.shape[-1]) for w in expert_ws)
    pieces = n_p // (2 * CHUNK)
    on_side = (len({w.shape for w in flat_ws}) == 1 and pieces >= 2 and tp % (2 * CHUNK) == 0
               and flat_ws[0].shape[0] % pieces == 0 and flat_ws[0].shape[0] // pieces <= CAST_MAX_ROWS)
    mix_p, st_p, *flat_wb = _mix_prompt(proj_p, lr_p, gup, gate_b, onw, swa_sinks[l],
                                        flat_ws if on_side else (), batch=bp, seq=tp)
    if not on_side:
        flat_wb = [w.astype(BF16) for w in flat_ws]
    gate_w16, up_w16, down_w16 = (wb.reshape(w.shape) for wb, w in zip(flat_wb, expert_ws))
    proj_s, lr_s, kv_s = _in_proj(hs, anw, w_main, w_lr)
    mix_s, st_s = _mix_sample(proj_s, lr_s, gup, gate_b, onw, swa_sinks[l],
                              jnp.swapaxes(state_gla[l], -1, -2),
                              cache_swa_k[l].reshape(bs, WINDOW, KB_W),
                              cache_swa_v[l].reshape(bs, WINDOW, KB_W), batch=bs, seq=ts)

    hp, hs, xn_all, lg_all = _out_proj(mix_p, hp, mix_s, hs, w_out[l].astype(BF16),
                                       ffn_norm_w[l].reshape(1, d), rw_hi, rw_lo,
                                       router_b[l].reshape(1, N_EXPERTS))

    weights, row_tok, row_slot, block_e, n_real = _route(lg_all, MOE_BM)
    y_slots = _moe_experts(xn_all, row_tok, row_slot, block_e, n_real,
                          gate_w16, expert_gate_b[l][:, None, :], up_w16, expert_up_b[l][:, None, :],
                          down_w16, expert_down_b[l][:, None, :], bm=MOE_BM)
    fin = final_norm_w.reshape(1, d)
    y_prompt = _combine(y_slots, weights, hp, fin, tok_off=0).reshape(bp, tp, d)
    y_sample = _combine(y_slots, weights, hs, fin, tok_off=n_p).reshape(bs, ts, d)

    kvp = kv_p.reshape(bp, tp, 2 * KB_W)[:, -WINDOW:].reshape(bp, WINDOW, 2, SWA_KV_HEADS, HEAD_DIM)
    kvs = kv_s.reshape(bs, ts, 2, SWA_KV_HEADS, HEAD_DIM)
    return (y_prompt, y_sample,
            jnp.swapaxes(st_p, -1, -2)[None], kvp[:, :, 0][None], kvp[:, :, 1][None],
            jnp.swapaxes(st_s, -1, -2)[None],
            jnp.concatenate([cache_swa_k[l], kvs[:, :, 0]], axis=1)[:, -WINDOW:][None],
            jnp.concatenate([cache_swa_v[l], kvs[:, :, 1]], axis=1)[:, -WINDOW:][None])
```

```python
import functools

import jax
import jax.numpy as jnp
from jax import lax
from jax.experimental import pallas as pl
from jax.experimental.pallas import tpu as pltpu

F32 = jnp.float32
BF16 = jnp.bfloat16

CHUNK = 64
GLA_HEADS = 4
GLA_DK = 128
GLA_DV = 256
GLA_GATE_RANK = 16
GLA_GATE_NORM = 16.0
SWA_Q_HEADS = 16
SWA_KV_HEADS = 4
SWA_GROUP = SWA_Q_HEADS // SWA_KV_HEADS
HEAD_DIM = 64
WINDOW = 128
WINDOW_CHUNKS = WINDOW // CHUNK
ALIBI_MAX_BIAS = 8.0
N_EXPERTS = 32
TOP_K = 4
SWIGLU_LIMIT = 7.0
SWIGLU_ALPHA = 1.702
RMS_EPS = 1e-5
NEG_INF = -1e30

QA_W = GLA_HEADS * GLA_DK
VA_W = GLA_HEADS * GLA_DV
QB_W = SWA_Q_HEADS * HEAD_DIM
KB_W = SWA_KV_HEADS * HEAD_DIM
MIX_W = VA_W + QB_W
MAIN_W = 2 * QA_W + 2 * VA_W + QB_W + 2 * KB_W
QB_COL = (2 * QA_W + 2 * VA_W) // QB_W
KB_COL = (2 * QA_W + 2 * VA_W + QB_W) // KB_W
LANE = 128
LR_PAD = LANE

PROJ_TN = 1536
PROJ_TM = 1024
ROW_TILE = 512
MOE_BM = 512
MOE_TF = 1024
CAST_MAX_ROWS = 256
OUT_TM = 256
COMBINE_TM = 256
ROUTE_TILE = 512
VMEM_LIMIT = 48 * 1024 * 1024
LARGE_VMEM_LIMIT = 58 * 1024 * 1024


def _cparams(sem, vmem=VMEM_LIMIT):
    return pltpu.CompilerParams(dimension_semantics=sem, vmem_limit_bytes=vmem)


def _dot(a, b):
    return jnp.dot(a, b, preferred_element_type=F32)


def _dot_tn(a, b):
    return lax.dot_general(a, b, (((0,), (0,)), ((), ())), preferred_element_type=F32)


def _dot_nt(a, b):
    return lax.dot_general(a, b, (((1,), (1,)), ((), ())), preferred_element_type=F32)


def _rms(x):
    return x * lax.rsqrt(jnp.mean(x * x, axis=-1, keepdims=True) + RMS_EPS)


def _hi_lo(x):
    hi = x.astype(BF16)
    return hi, (x - hi.astype(F32)).astype(BF16)


def _in_proj_kernel(x_ref, nw_ref, w_ref, wlr_ref, out_ref, lr_ref, kv_ref, xn_ref, *, n_col_steps):
    j = pl.program_id(1)

    @pl.when(j == 0)
    def _():
        xnb = (_rms(x_ref[...]) * nw_ref[...]).astype(BF16)
        xn_ref[...] = xnb
        lr_ref[...] = _dot(xnb, wlr_ref[...]).astype(BF16)

    acc = _dot(xn_ref[...], w_ref[...])
    out_ref[...] = acc.astype(BF16)

    @pl.when(j == n_col_steps - 1)
    def _():
        kv_ref[...] = acc[:, PROJ_TN - 2 * KB_W:]


def _in_proj(x2d, norm_w, w_main, w_lr):
    n, d = x2d.shape
    tm = min(PROJ_TM, n)
    n_col_steps = MAIN_W // PROJ_TN
    return pl.pallas_call(
        functools.partial(_in_proj_kernel, n_col_steps=n_col_steps),
        out_shape=(jax.ShapeDtypeStruct((n, MAIN_W), BF16),
                   jax.ShapeDtypeStruct((n, LR_PAD), BF16),
                   jax.ShapeDtypeStruct((n, 2 * KB_W), F32)),
        grid=(n // tm, n_col_steps),
        in_specs=[pl.BlockSpec((tm, d), lambda i, j: (i, 0)),
                  pl.BlockSpec((1, d), lambda i, j: (0, 0)),
                  pl.BlockSpec((d, PROJ_TN), lambda i, j: (0, j)),
                  pl.BlockSpec((d, LR_PAD), lambda i, j: (0, 0))],
        out_specs=(pl.BlockSpec((tm, PROJ_TN), lambda i, j: (i, j)),
                   pl.BlockSpec((tm, LR_PAD), lambda i, j: (i, 0)),
                   pl.BlockSpec((tm, 2 * KB_W), lambda i, j: (i, 0))),
        scratch_shapes=[pltpu.VMEM((tm, d), BF16)],
        compiler_params=_cparams(("parallel", "arbitrary"), LARGE_VMEM_LIMIT),
        name="in_proj",
    )(x2d, norm_w, w_main, w_lr)


def _gla_chunk(rows, chunk, q_ref, k_ref, v_ref, ga_ref, lr_ref, gup, gb, nw, tri, s_scr, o_ref):
    z = _dot(lr_ref[rows, :], gup) + gb
    fc = (jnp.minimum(z, 0.0) - jnp.log1p(jnp.exp(-jnp.abs(z)))) * (1.0 / GLA_GATE_NORM)
    hi, lo = _hi_lo(fc)
    g = _dot(tri, hi) + _dot(tri, lo)
    g_end = g[chunk - 1:chunk, :]
    k_dec = (k_ref[rows, :].astype(F32) * jnp.exp(g_end - g)).astype(BF16)
    a_end = jnp.exp(g_end)
    for h in range(GLA_HEADS):
        ks = slice(h * GLA_DK, (h + 1) * GLA_DK)
        vs = slice(h * GLA_DV, (h + 1) * GLA_DV)
        upd = _dot_tn(v_ref[rows, vs], k_dec[:, ks])
        s_new = s_scr[h] * a_end[:, ks] + upd
        s_scr[h] = s_new
        o = _dot_nt(q_ref[rows, ks], s_new.astype(BF16)) * (GLA_DK ** -0.5)
        gate = ga_ref[rows, vs].astype(F32)
        o_ref[rows, vs] = (_rms(o) * nw * (gate * jax.nn.sigmoid(gate))).astype(BF16)


def _alibi_slopes():
    s = jnp.exp2(-ALIBI_MAX_BIAS * jnp.arange(1, SWA_Q_HEADS + 1, dtype=F32) / SWA_Q_HEADS)
    return s.reshape(SWA_KV_HEADS, SWA_GROUP)


def _stacked_bias(q_off, lq, lk):
    dist = jnp.abs(q_off + jnp.arange(lq)[:, None] - jnp.arange(lk)[None, :]).astype(F32)
    b = -_alibi_slopes()[:, :, None, None] * dist
    return b.reshape(SWA_KV_HEADS, SWA_GROUP * lq, lk)


def _stacked_sinks(sinks, lq):
    s = sinks.astype(F32).reshape(SWA_KV_HEADS, SWA_GROUP, 1)
    return jnp.broadcast_to(s, (SWA_KV_HEADS, SWA_GROUP, lq)).reshape(SWA_KV_HEADS, SWA_GROUP * lq, 1)


def _attend_group(q, keys, vals, bias, sink, valid):
    lq = q.shape[0]
    qs = jnp.concatenate([q[:, j * HEAD_DIM:(j + 1) * HEAD_DIM] for j in range(SWA_GROUP)], axis=0)
    s = _dot_nt(qs, keys) * (HEAD_DIM ** -0.5) + bias
    if valid is not None:
        s = jnp.where(valid, s, NEG_INF)
    m = jnp.maximum(jnp.max(s, axis=-1, keepdims=True), sink)
    p = jnp.exp(s - m)
    denom = jnp.sum(p, axis=-1, keepdims=True) + jnp.exp(sink - m)
    o = _dot(p.astype(BF16), vals) / denom
    return jnp.concatenate([o[j * lq:(j + 1) * lq, :] for j in range(SWA_GROUP)], axis=1)


def _mix_prompt_kernel(*refs, n_chunks, n_steps, n_cast, cast_rows):
    (q_ref, k_ref, v_ref, ga_ref, lr_ref, qb_ref, kc_ref, vc_ref, kp_ref, vp_ref,
     gup_ref, gb_ref, nw_ref, tri_ref, bias_ref, sink_ref) = refs[:16]
    w_hbm = refs[16:16 + n_cast]
    o_ref, st_ref = refs[16 + n_cast:18 + n_cast]
    wb_hbm = refs[18 + n_cast:18 + 2 * n_cast]
    s_scr, k_scr, v_scr = refs[18 + 2 * n_cast:21 + 2 * n_cast]
    if n_cast:
        win, wout, isem, osem = refs[21 + 2 * n_cast:]
    t = pl.program_id(1)
    back = WINDOW_CHUNKS * CHUNK
    span = back + CHUNK
    pairs = n_chunks // 2
    step_id = pl.program_id(0) * pl.num_programs(1) + t
    total = n_steps * pairs

    def w_in(j, piece, slot):
        return pltpu.make_async_copy(w_hbm[j].at[pl.ds(piece * cast_rows, cast_rows)],
                                     win.at[slot, j], isem.at[slot])

    def w_out(j, piece, slot):
        return pltpu.make_async_copy(wout.at[slot, j],
                                     wb_hbm[j].at[pl.ds(piece * cast_rows, cast_rows)], osem.at[slot])

    @pl.when(t == 0)
    def _():
        s_scr[...] = jnp.zeros_like(s_scr)

    k_scr[0:back, :] = kp_ref[...]
    v_scr[0:back, :] = vp_ref[...]
    k_scr[back:, :] = kc_ref[...]
    v_scr[back:, :] = vc_ref[...]
    key_idx = lax.broadcasted_iota(jnp.int32, (SWA_GROUP * CHUNK, span), 1)
    gup, gb, nw, tri = gup_ref[...], gb_ref[...], nw_ref[...], tri_ref[...]

    def chunk_step(c):
        r0 = pl.multiple_of(c * CHUNK, CHUNK)
        rows = pl.ds(r0, CHUNK)
        _gla_chunk(rows, CHUNK, q_ref, k_ref, v_ref, ga_ref, lr_ref, gup, gb, nw, tri, s_scr, o_ref)
        valid = jnp.logical_or(t > 0, key_idx >= back - c * CHUNK)
        for g in range(SWA_KV_HEADS):
            hs = slice(g * HEAD_DIM, (g + 1) * HEAD_DIM)
            qs = slice(g * SWA_GROUP * HEAD_DIM, (g + 1) * SWA_GROUP * HEAD_DIM)
            o = _attend_group(qb_ref[rows, qs], k_scr[pl.ds(r0, span), hs],
                              v_scr[pl.ds(r0, span), hs], bias_ref[g], sink_ref[g], valid)
            o_ref[rows, VA_W + g * SWA_GROUP * HEAD_DIM:VA_W + (g + 1) * SWA_GROUP * HEAD_DIM] = o.astype(BF16)

    def pair_step(p, carry):
        piece = step_id * pairs + p
        slot = piece % 2
        if n_cast:
            @pl.when(piece == 0)
            def _():
                for j in range(n_cast):
                    w_in(j, 0, 0).start()

            for j in range(n_cast):
                w_in(j, piece, slot).wait()

            @pl.when(piece + 1 < total)
            def _():
                for j in range(n_cast):
                    w_in(j, piece + 1, 1 - slot).start()

            @pl.when(piece >= 2)
            def _():
                for j in range(n_cast):
                    w_out(j, piece - 2, slot).wait()

        chunk_step(2 * p)
        chunk_step(2 * p + 1)
        if n_cast:
            for j in range(n_cast):
                wout[slot, j] = win[slot, j].astype(BF16)
            for j in range(n_cast):
                w_out(j, piece, slot).start()
        return carry

    lax.fori_loop(0, pairs, pair_step, 0)

    @pl.when(t == pl.num_programs(1) - 1)
    def _():
        st_ref[0] = s_scr[...]

    if n_cast:
        @pl.when(step_id == n_steps - 1)
        def _():
            for piece in (total - 2, total - 1):
                for j in range(n_cast):
                    w_out(j, piece, piece % 2).wait()


def _gla_consts(gup, gate_b, norm_w, chunk):
    tri = (jnp.arange(chunk)[:, None] >= jnp.arange(chunk)[None, :]).astype(BF16)
    specs = [pl.BlockSpec((LR_PAD, QA_W), lambda *_: (0, 0)),
             pl.BlockSpec((1, QA_W), lambda *_: (0, 0)),
             pl.BlockSpec((1, GLA_DV), lambda *_: (0, 0)),
             pl.BlockSpec((chunk, chunk), lambda *_: (0, 0))]
    return specs, [gup, gate_b, norm_w, tri]


def _mix_prompt(proj, lr, gup, gate_b, norm_w, sinks, cast_ws=(), *, batch, seq):
    rows = min(ROW_TILE, seq)
    n_chunks = rows // CHUNK
    nt = seq // rows
    back = WINDOW_CHUNKS * CHUNK
    pb = rows // back
    bias = _stacked_bias(back, CHUNK, back + CHUNK)
    sink = _stacked_sinks(sinks, CHUNK)
    rb = lambda b, t: b * nt + t
    prev = lambda b, t: jnp.maximum(rb(b, t) * pb - 1, 0)
    cspecs, cargs = _gla_consts(gup, gate_b, norm_w, CHUNK)
    n_cast = len(cast_ws)
    cast_rows, cast_scratch = 0, []
    if n_cast:
        assert n_chunks % 2 == 0
        w_rows, w_cols = cast_ws[0].shape
        pieces = batch * nt * (n_chunks // 2)
        assert all(w.shape == (w_rows, w_cols) for w in cast_ws) and w_rows % pieces == 0 and pieces >= 2
        cast_rows = w_rows // pieces
        cast_scratch = [pltpu.VMEM((2, n_cast, cast_rows, w_cols), F32),
                        pltpu.VMEM((2, n_cast, cast_rows, w_cols), BF16),
                        pltpu.SemaphoreType.DMA((2,)), pltpu.SemaphoreType.DMA((2,))]
    any_spec = pl.BlockSpec(memory_space=pl.ANY)
    return pl.pallas_call(
        functools.partial(_mix_prompt_kernel, n_chunks=n_chunks, n_steps=batch * nt, n_cast=n_cast,
                          cast_rows=cast_rows),
        out_shape=(jax.ShapeDtypeStruct((batch * seq, MIX_W), BF16),
                   jax.ShapeDtypeStruct((batch, GLA_HEADS, GLA_DV, GLA_DK), F32),
                   *[jax.ShapeDtypeStruct(w.shape, BF16) for w in cast_ws]),
        grid=(batch, nt),
        in_specs=[pl.BlockSpec((rows, QA_W), lambda b, t: (rb(b, t), 0)),
                  pl.BlockSpec((rows, QA_W), lambda b, t: (rb(b, t), 1)),
                  pl.BlockSpec((rows, VA_W), lambda b, t: (rb(b, t), 1)),
                  pl.BlockSpec((rows, VA_W), lambda b, t: (rb(b, t), 2)),
                  pl.BlockSpec((rows, LR_PAD), lambda b, t: (rb(b, t), 0)),
                  pl.BlockSpec((rows, QB_W), lambda b, t: (rb(b, t), QB_COL)),
                  pl.BlockSpec((rows, KB_W), lambda b, t: (rb(b, t), KB_COL)),
                  pl.BlockSpec((rows, KB_W), lambda b, t: (rb(b, t), KB_COL + 1)),
                  pl.BlockSpec((back, KB_W), lambda b, t: (prev(b, t), KB_COL)),
                  pl.BlockSpec((back, KB_W), lambda b, t: (prev(b, t), KB_COL + 1)),
                  *cspecs,
                  pl.BlockSpec(bias.shape, lambda b, t: (0, 0, 0)),
                  pl.BlockSpec(sink.shape, lambda b, t: (0, 0, 0)),
                  *[any_spec] * n_cast],
        out_specs=(pl.BlockSpec((rows, MIX_W), lambda b, t: (rb(b, t), 0)),
                   pl.BlockSpec((1, GLA_HEADS, GLA_DV, GLA_DK), lambda b, t: (b, 0, 0, 0)),
                   *[any_spec] * n_cast),
        scratch_shapes=[pltpu.VMEM((GLA_HEADS, GLA_DV, GLA_DK), F32),
                        pltpu.VMEM((rows + back, KB_W), BF16),
                        pltpu.VMEM((rows + back, KB_W), BF16),
                        *cast_scratch],
        compiler_params=_cparams(("arbitrary", "arbitrary")),
        name="mix_prompt",
    )(proj, proj, proj, proj, lr, proj, proj, proj, proj, proj, *cargs, bias, sink, *cast_ws)


def _mix_sample_kernel(q_ref, k_ref, v_ref, ga_ref, lr_ref, qb_ref, kn_ref, vn_ref, wk_ref, wv_ref,
                       s0_ref, gup_ref, gb_ref, nw_ref, tri_ref, bias_ref, sink_ref,
                       o_ref, st_ref, s_scr, *, seq):
    s_scr[...] = s0_ref[0]
    _gla_chunk(slice(0, seq), seq, q_ref, k_ref, v_ref, ga_ref, lr_ref,
               gup_ref[...], gb_ref[...], nw_ref[...], tri_ref[...], s_scr, o_ref)
    st_ref[0] = s_scr[...]
    keys = jnp.concatenate([wk_ref[0].astype(BF16), kn_ref[...]], axis=0)
    vals = jnp.concatenate([wv_ref[0].astype(BF16), vn_ref[...]], axis=0)
    for g in range(SWA_KV_HEADS):
        hs = slice(g * HEAD_DIM, (g + 1) * HEAD_DIM)
        qs = slice(g * SWA_GROUP * HEAD_DIM, (g + 1) * SWA_GROUP * HEAD_DIM)
        o = _attend_group(qb_ref[:, qs], keys[:, hs], vals[:, hs], bias_ref[g], sink_ref[g], None)
        o_ref[:, VA_W + g * SWA_GROUP * HEAD_DIM:VA_W + (g + 1) * SWA_GROUP * HEAD_DIM] = o.astype(BF16)


def _mix_sample(proj, lr, gup, gate_b, norm_w, sinks, s0_t, win_k, win_v, *, batch, seq):
    bias = _stacked_bias(WINDOW, seq, WINDOW + seq)
    sink = _stacked_sinks(sinks, seq)
    cspecs, cargs = _gla_consts(gup, gate_b, norm_w, seq)
    state_spec = pl.BlockSpec((1, GLA_HEADS, GLA_DV, GLA_DK), lambda b: (b, 0, 0, 0))
    return pl.pallas_call(
        functools.partial(_mix_sample_kernel, seq=seq),
        out_shape=(jax.ShapeDtypeStruct((batch * seq, MIX_W), BF16),
                   jax.ShapeDtypeStruct((batch, GLA_HEADS, GLA_DV, GLA_DK), F32)),
        grid=(batch,),
        in_specs=[pl.BlockSpec((seq, QA_W), lambda b: (b, 0)),
                  pl.BlockSpec((seq, QA_W), lambda b: (b, 1)),
                  pl.BlockSpec((seq, VA_W), lambda b: (b, 1)),
                  pl.BlockSpec((seq, VA_W), lambda b: (b, 2)),
                  pl.BlockSpec((seq, LR_PAD), lambda b: (b, 0)),
                  pl.BlockSpec((seq, QB_W), lambda b: (b, QB_COL)),
                  pl.BlockSpec((seq, KB_W), lambda b: (b, KB_COL)),
                  pl.BlockSpec((seq, KB_W), lambda b: (b, KB_COL + 1)),
                  pl.BlockSpec((1, WINDOW, KB_W), lambda b: (b, 0, 0)),
                  pl.BlockSpec((1, WINDOW, KB_W), lambda b: (b, 0, 0)),
                  state_spec,
                  *cspecs,
                  pl.BlockSpec(bias.shape, lambda b: (0, 0, 0)),
                  pl.BlockSpec(sink.shape, lambda b: (0, 0, 0))],
        out_specs=(pl.BlockSpec((seq, MIX_W), lambda b: (b, 0)), state_spec),
        scratch_shapes=[pltpu.VMEM((GLA_HEADS, GLA_DV, GLA_DK), F32)],
        compiler_params=_cparams(("parallel",)),
        name="mix_sample",
    )(proj, proj, proj, proj, lr, proj, proj, proj, win_k, win_v, s0_t, *cargs, bias, sink)


def _out_proj_kernel(mp_ref, xp_ref, ms_ref, xs_ref, w_ref, nw_ref, rwh_ref, rwl_ref, rb_ref,
                     hp_ref, hs_ref, xn_ref, lg_ref, *, p_tiles):
    i = pl.program_id(0)

    def tile(m_ref, x_ref, h_ref):
        h = x_ref[...] + _dot(m_ref[...], w_ref[...])
        h_ref[...] = h
        xn = _rms(h) * nw_ref[...]
        xn_ref[...] = xn
        xh, xl = _hi_lo(xn)
        rwh = rwh_ref[...]
        lg_ref[...] = _dot(xh, rwh) + _dot(xl, rwh) + _dot(xh, rwl_ref[...]) + rb_ref[...]

    @pl.when(i < p_tiles)
    def _():
        tile(mp_ref, xp_ref, hp_ref)

    @pl.when(i >= p_tiles)
    def _():
        tile(ms_ref, xs_ref, hs_ref)


def _out_proj(mix_p, x_p, mix_s, x_s, w_out, norm_w, rw_hi, rw_lo, router_b):
    n_p, d = x_p.shape
    n_s = x_s.shape[0]
    tm = min(OUT_TM, n_p, n_s)
    p_tiles, s_tiles = n_p // tm, n_s // tm
    n_all = n_p + n_s
    pidx = lambda i: (jnp.minimum(i, p_tiles - 1), 0)
    sidx = lambda i: (jnp.maximum(i - p_tiles, 0), 0)
    const = lambda i: (0, 0)
    return pl.pallas_call(
        functools.partial(_out_proj_kernel, p_tiles=p_tiles),
        out_shape=(jax.ShapeDtypeStruct((n_p, d), F32),
                   jax.ShapeDtypeStruct((n_s, d), F32),
                   jax.ShapeDtypeStruct((n_all, d), F32),
                   jax.ShapeDtypeStruct((n_all, N_EXPERTS), F32)),
        grid=(p_tiles + s_tiles,),
        in_specs=[pl.BlockSpec((tm, MIX_W), pidx),
                  pl.BlockSpec((tm, d), pidx),
                  pl.BlockSpec((tm, MIX_W), sidx),
                  pl.BlockSpec((tm, d), sidx),
                  pl.BlockSpec((MIX_W, d), const),
                  pl.BlockSpec((1, d), const),
                  pl.BlockSpec((d, N_EXPERTS), const),
                  pl.BlockSpec((d, N_EXPERTS), const),
                  pl.BlockSpec((1, N_EXPERTS), const)],
        out_specs=(pl.BlockSpec((tm, d), pidx),
                   pl.BlockSpec((tm, d), sidx),
                   pl.BlockSpec((tm, d), lambda i: (i, 0)),
                   pl.BlockSpec((tm, N_EXPERTS), lambda i: (i, 0))),
        compiler_params=_cparams(("arbitrary",)),
        name="out_proj",
    )(mix_p, x_p, mix_s, x_s, w_out, norm_w, rw_hi, rw_lo, router_b)


def _route(logits, bm):
    n = logits.shape[0]
    tk = n * TOP_K
    top_val, top_idx = lax.top_k(logits, TOP_K)
    weights = jax.nn.softmax(top_val, axis=-1)
    flat_e = top_idx.reshape(tk).astype(jnp.int32)
    assert tk % ROUTE_TILE == 0
    onehot = (flat_e[:, None] == jnp.arange(N_EXPERTS, dtype=jnp.int32)[None, :])
    onehot = onehot.reshape(tk // ROUTE_TILE, ROUTE_TILE, N_EXPERTS)
    lower = (jnp.arange(ROUTE_TILE)[:, None] > jnp.arange(ROUTE_TILE)[None, :]).astype(BF16)
    within = jnp.einsum('ij,tje->tie', lower, onehot.astype(BF16), preferred_element_type=F32)
    totals = jnp.sum(onehot.astype(F32), axis=1)
    before = jnp.cumsum(totals, axis=0) - totals
    counts = (before[-1] + totals[-1]).astype(jnp.int32)
    padded = (counts + bm - 1) // bm * bm
    pad_end = jnp.cumsum(padded)
    pad_start = pad_end - padded
    row_of = within + before[:, None, :] + pad_start.astype(F32)[None, None, :]
    dest = jnp.sum(jnp.where(onehot, row_of, 0.0), axis=-1).astype(jnp.int32).reshape(tk)
    n_blocks = -(-(tk + N_EXPERTS * (bm - 1)) // bm)
    n_rows = n_blocks * bm
    block_e = jnp.sum(pad_end[None, :] <= (jnp.arange(n_blocks, dtype=jnp.int32) * bm)[:, None], axis=1)
    block_e = jnp.minimum(block_e, N_EXPERTS - 1).astype(jnp.int32)
    asg = jnp.arange(tk, dtype=jnp.int32)
    row_asg = jnp.full((n_rows,), -1, jnp.int32).at[dest].set(asg, unique_indices=True,
                                                              mode='promise_in_bounds')
    is_pad = row_asg < 0
    row_tok = jnp.where(is_pad, 0, row_asg // TOP_K)
    row_slot = jnp.where(is_pad, tk + jnp.arange(n_rows, dtype=jnp.int32) % bm,
                         (row_asg % TOP_K) * n + row_asg // TOP_K)
    n_real = (pad_end[-1] // bm).astype(jnp.int32).reshape(1)
    return weights, row_tok, row_slot, block_e, n_real


def _row_copy(src_ref, dst_ref, sem, src_row, dst_row, n_rows=1):
    return pltpu.make_async_copy(src_ref.at[pl.ds(src_row, n_rows)],
                                 dst_ref.at[pl.ds(dst_row, n_rows)], sem)


def _moe_kernel(be_ref, nreal_ref, idx_ref, idx_next_ref, slot_prev_ref, x_hbm,
                gw_ref, gb_ref, uw_ref, ub_ref, dw_ref, db_ref, ys_hbm,
                acc_ref, xb_ref, xg_ref, yb_ref, gsem, ssem, *, bm, nf):
    blk = pl.program_id(0)
    f = pl.program_id(1)
    n_real = nreal_ref[0]
    cur = blk % 2

    def gather(rows_ref, i):
        return _row_copy(x_hbm, xg_ref, gsem, rows_ref[0, 0, i], i)

    def scatter(i):
        return _row_copy(yb_ref, ys_hbm, ssem, i, slot_prev_ref[0, 0, i])

    def wait_gather():
        _row_copy(x_hbm, xg_ref, gsem, 0, 0, bm).wait()

    def wait_scatter():
        _row_copy(yb_ref, ys_hbm, ssem, 0, 0, bm).wait()

    def start_all(make):
        def body(i, carry):
            make(i).start()
            return carry
        lax.fori_loop(0, bm, body, 0)

    def cast_rows(slot):
        xb_ref[slot] = xg_ref[...].astype(BF16)

    def step(fs):
        if fs == 0:
            @pl.when(blk == 0)
            def _():
                yb_ref[...] = jnp.zeros_like(yb_ref)
                start_all(lambda i: gather(idx_ref, i))
                wait_gather()
                cast_rows(0)

            for i in range(bm):
                gather(idx_next_ref, i).start()
        if fs == nf - 1:
            wait_gather()
            for i in range(bm):
                scatter(i).start(priority=i % 2)
            cast_rows(1 - cur)

        xb = xb_ref[cur]
        g = jnp.minimum(_dot(xb, gw_ref[0]) + gb_ref[0], SWIGLU_LIMIT)
        u = jnp.clip(_dot(xb, uw_ref[0]) + ub_ref[0], -SWIGLU_LIMIT, SWIGLU_LIMIT)
        hmid = g * jax.nn.sigmoid(SWIGLU_ALPHA * g) * (u + 1.0)
        part = _dot(hmid.astype(BF16), dw_ref[0])
        if fs > 0:
            part = acc_ref[...] + part
        if fs < nf - 1:
            acc_ref[...] = part
        else:
            wait_scatter()
            yb_ref[...] = part + db_ref[0]

    for fs in range(nf):
        pl.when(jnp.logical_and(blk < n_real, f == fs))(functools.partial(step, fs))

    @pl.when(jnp.logical_and(blk == n_real, f == 0))
    def _():
        start_all(scatter)
        wait_scatter()


def _moe_experts(x_all, row_tok, row_slot, block_e, n_real, gate_w, gate_b, up_w, up_b, down_w, down_b,
                 *, bm):
    n_rows = row_tok.shape[0]
    n, d = x_all.shape
    n_blocks = n_rows // bm
    d_ff = gate_w.shape[-1]
    tf = min(MOE_TF, d_ff)
    nf = d_ff // tf
    assert nf >= 2, "the gather started in the first d_ff step is drained in the last one"
    idx = row_tok.reshape(n_blocks, 1, bm)
    trash = TOP_K * n + jnp.arange(bm, dtype=jnp.int32)
    slot_prev = jnp.concatenate([trash, row_slot]).reshape(n_blocks + 1, 1, bm)

    def e_of(b, be, nr):
        return be[jnp.minimum(b, nr[0] - 1)]

    def f_of(b, f, nr):
        return jnp.where(b < nr[0], f, nf - 1)

    smem = functools.partial(pl.BlockSpec, (1, 1, bm), memory_space=pltpu.SMEM)
    return pl.pallas_call(
        functools.partial(_moe_kernel, bm=bm, nf=nf),
        out_shape=jax.ShapeDtypeStruct((TOP_K * n + bm, d), F32),
        grid_spec=pltpu.PrefetchScalarGridSpec(
            num_scalar_prefetch=2,
            grid=(n_blocks, nf),
            in_specs=[
                smem(lambda b, f, be, nr: (b, 0, 0)),
                smem(lambda b, f, be, nr: (jnp.minimum(b + 1, n_blocks - 1), 0, 0)),
                smem(lambda b, f, be, nr: (b, 0, 0)),
                pl.BlockSpec(memory_space=pl.ANY),
                pl.BlockSpec((1, d, tf), lambda b, f, be, nr: (e_of(b, be, nr), 0, f_of(b, f, nr))),
                pl.BlockSpec((1, 1, tf), lambda b, f, be, nr: (e_of(b, be, nr), 0, f_of(b, f, nr))),
                pl.BlockSpec((1, d, tf), lambda b, f, be, nr: (e_of(b, be, nr), 0, f_of(b, f, nr))),
                pl.BlockSpec((1, 1, tf), lambda b, f, be, nr: (e_of(b, be, nr), 0, f_of(b, f, nr))),
                pl.BlockSpec((1, tf, d), lambda b, f, be, nr: (e_of(b, be, nr), f_of(b, f, nr), 0)),
                pl.BlockSpec((1, 1, d), lambda b, f, be, nr: (e_of(b, be, nr), 0, 0)),
            ],
            out_specs=pl.BlockSpec(memory_space=pl.ANY),
            scratch_shapes=[pltpu.VMEM((bm, d), F32), pltpu.VMEM((2, bm, d), BF16),
                            pltpu.VMEM((bm, d), F32), pltpu.VMEM((bm, d), F32),
                            pltpu.SemaphoreType.DMA, pltpu.SemaphoreType.DMA]),
        compiler_params=_cparams(("arbitrary", "arbitrary"), LARGE_VMEM_LIMIT),
        name="moe_experts",
    )(block_e, n_real, idx, idx, slot_prev, x_all, gate_w, gate_b, up_w, up_b, down_w, down_b)


def _combine_kernel(*refs):
    y_refs, (w_ref, h_ref, nw_ref, out_ref) = refs[:TOP_K], refs[TOP_K:]
    w = w_ref[...]
    y = w[:, 0:1] * y_refs[0][...]
    for k in range(1, TOP_K):
        y = y + w[:, k:k + 1] * y_refs[k][...]
    out_ref[...] = _rms(h_ref[...] + y) * nw_ref[...]


def _combine(y_slots, weights, h, final_norm_w, *, tok_off):
    n, d = h.shape
    n_all = weights.shape[0]
    tm = min(COMBINE_TM, n)
    ob = tok_off // tm
    slot_blocks = n_all // tm
    y_specs = [pl.BlockSpec((tm, d), functools.partial(lambda k, i: (k * slot_blocks + ob + i, 0), k))
               for k in range(TOP_K)]
    return pl.pallas_call(
        _combine_kernel,
        out_shape=jax.ShapeDtypeStruct((n, d), F32),
        grid=(n // tm,),
        in_specs=[*y_specs,
                  pl.BlockSpec((tm, TOP_K), lambda i: (i + ob, 0)),
                  pl.BlockSpec((tm, d), lambda i: (i, 0)),
                  pl.BlockSpec((1, d), lambda i: (0, 0))],
        out_specs=pl.BlockSpec((tm, d), lambda i: (i, 0)),
        compiler_params=_cparams(("parallel",)),
        name="moe_combine",
    )(*([y_slots] * TOP_K), weights, h, final_norm_w)


def _split_w_in(w_in):
    lr0 = 2 * QA_W + VA_W
    w_main = jnp.concatenate([w_in[:, :lr0], w_in[:, lr0 + GLA_GATE_RANK:]], axis=1).astype(BF16)
    w_lr = jnp.pad(w_in[:, lr0:lr0 + GLA_GATE_RANK], ((0, 0), (0, LR_PAD - GLA_GATE_RANK))).astype(BF16)
    return w_main, w_lr


def kernel(x_prompt, x_sample, state_gla, cache_swa_k, cache_swa_v, attn_norm_w, w_in, gla_gate_up_w, gla_gate_b, gla_out_norm_w, swa_sinks, w_out, ffn_norm_w, router_w, router_b, expert_gate_w, expert_gate_b, expert_up_w, expert_up_b, expert_down_w, expert_down_b, final_norm_w):
    depth = w_in.shape[0]
    assert depth == 1, "the combine applies the final norm, so exactly one layer is supported"
    bp, tp, d = x_prompt.shape
    bs, ts, _ = x_sample.shape
    n_p, n_s = bp * tp, bs * ts
    hp = x_prompt.reshape(n_p, d)
    hs = x_sample.reshape(n_s, d)
    l = 0
    w_main, w_lr = _split_w_in(w_in[l])
    gup = jnp.pad(gla_gate_up_w[l], ((0, LR_PAD - GLA_GATE_RANK), (0, 0))).astype(BF16)
    gate_b = gla_gate_b[l].reshape(1, QA_W)
    onw = gla_out_norm_w[l].reshape(1, GLA_DV)
    anw = attn_norm_w[l].reshape(1, d)
    rw_hi = router_w[l].astype(BF16)
    rw_lo = (router_w[l] - rw_hi.astype(F32)).astype(BF16)

    proj_p, lr_p, kv_p = _in_proj(hp, anw, w_main, w_lr)
    expert_ws = (expert_gate_w[l], expert_up_w[l], expert_down_w[l])
    flat_ws = tuple(w.reshape(-1, w.shape[-1]) for w in expert_ws)
    pieces = n_p // (2 * CHUNK)
    on_side = (len({w.shape for w in flat_ws}) == 1 and pieces >= 2 and tp % (2 * CHUNK) == 0
               and flat_ws[0].shape[0] % pieces == 0 and flat_ws[0].shape[0] // pieces <= CAST_MAX_ROWS)
    mix_p, st_p, *flat_wb = _mix_prompt(proj_p, lr_p, gup, gate_b, onw, swa_sinks[l],
                                        flat_ws if on_side else (), batch=bp, seq=tp)
    if not on_side:
        flat_wb = [w.astype(BF16) for w in flat_ws]
    gate_w16, up_w16, down_w16 = (wb.reshape(w.shape) for wb, w in zip(flat_wb, expert_ws))
    proj_s, lr_s, kv_s = _in_proj(hs, anw, w_main, w_lr)
    mix_s, st_s = _mix_sample(proj_s, lr_s, gup, gate_b, onw, swa_sinks[l],
                              jnp.swapaxes(state_gla[l], -1, -2),
                              cache_swa_k[l].reshape(bs, WINDOW, KB_W),
                              cache_swa_v[l].reshape(bs, WINDOW, KB_W), batch=bs, seq=ts)

    hp, hs, xn_all, lg_all = _out_proj(mix_p, hp, mix_s, hs, w_out[l].astype(BF16),
                                       ffn_norm_w[l].reshape(1, d), rw_hi, rw_lo,
                                       router_b[l].reshape(1, N_EXPERTS))

    weights, row_tok, row_slot, block_e, n_real = _route(lg_all, MOE_BM)
    y_slots = _moe_experts(xn_all, row_tok, row_slot, block_e, n_real,
                          gate_w16, expert_gate_b[l][:, None, :], up_w16, expert_up_b[l][:, None, :],
                          down_w16, expert_down_b[l][:, None, :], bm=MOE_BM)
    fin = final_norm_w.reshape(1, d)
    y_prompt = _combine(y_slots, weights, hp, fin, tok_off=0).reshape(bp, tp, d)
    y_sample = _combine(y_slots, weights, hs, fin, tok_off=n_p).reshape(bs, ts, d)

    kvp = kv_p.reshape(bp, tp, 2 * KB_W)[:, -WINDOW:].reshape(bp, WINDOW, 2, SWA_KV_HEADS, HEAD_DIM)
    kvs = kv_s.reshape(bs, ts, 2, SWA_KV_HEADS, HEAD_DIM)
    return (y_prompt, y_sample,
            jnp.swapaxes(st_p, -1, -2)[None], kvp[:, :, 0][None], kvp[:, :, 1][None],
            jnp.swapaxes(st_s, -1, -2)[None],
            jnp.concatenate([cache_swa_k[l], kvs[:, :, 0]], axis=1)[:, -WINDOW:][None],
            jnp.concatenate([cache_swa_v[l], kvs[:, :, 1]], axis=1)[:, -WINDOW:][None])
```
